```python
import numpy as np
import jax
import jax.numpy as jnp
from jax import lax

D_MODEL = 1024
BATCH = 4
SEQ = 4096
DEPTH = 1
DEC_BATCH = 32
DEC_SEQ = 1
PAST_LEN = 16384
PAGE_SIZE = 128

HEAD_DIM = 128
ATTN_HEADS = D_MODEL // HEAD_DIM
D_ATTN = ATTN_HEADS * HEAD_DIM
IDX_HEADS = 4
IDX_DIM = 64
TOPK_MAX = 256
GLA_HEADS = 4
GLA_DK = D_MODEL // 2
GLA_DV = D_MODEL
GLA_DK_HEAD = GLA_DK // GLA_HEADS
GLA_DV_HEAD = GLA_DV // GLA_HEADS
GATE_RANK = 16
GATE_TEMP = 16.0
GLA_CHUNK = 64
Q_BLOCK = 64
ROPE_THETA = 10000.0
NORM_EPS = 1e-6
NEG_INF = -1e30
SPLIT_SIZES = (D_ATTN, D_ATTN, D_ATTN, D_ATTN, IDX_HEADS * IDX_DIM, IDX_DIM, IDX_HEADS,
               GLA_DK, GLA_DK, GLA_DV, GATE_RANK, GLA_DV, D_MODEL, D_MODEL)
IN_WIDTH = (4 * D_ATTN + IDX_HEADS * IDX_DIM + IDX_DIM + IDX_HEADS
            + 2 * GLA_DK + 2 * GLA_DV + GATE_RANK + 2 * D_MODEL)

kernel_name = 'dsa_gla_gated_hybrid_step'


def rms_norm(x, g):
    xf = x.astype(jnp.float32)
    y = xf * lax.rsqrt(jnp.mean(xf * xf, axis=-1, keepdims=True) + NORM_EPS)
    return (y * g.astype(jnp.float32)).astype(x.dtype)


def rotary(x, pos):
    half = x.shape[-1] // 2
    inv_freq = ROPE_THETA ** (-jnp.arange(half, dtype=jnp.float32) / half)
    ang = pos.astype(jnp.float32)[:, None] * inv_freq[None, :]
    cos = jnp.cos(ang)[:, None, :]
    sin = jnp.sin(ang)[:, None, :]
    xf = x.astype(jnp.float32)
    x1, x2 = xf[..., :half], xf[..., half:]
    return jnp.concatenate([x1 * cos - x2 * sin, x2 * cos + x1 * sin], axis=-1).astype(x.dtype)


def gather_rows(arr, idx):
    return jax.vmap(lambda a, i: a[i])(arr, idx)


def project(x, pos, p):
    B_, T, _ = x.shape
    h = rms_norm(x, p['pre_norm_g'])
    z = jnp.einsum('btd,de->bte', h, p['w_in'])
    offs = [int(o) for o in np.cumsum(SPLIT_SIZES)[:-1]]
    (q_a, k_a, v_a, z_a, q_i, k_i, w_i, q_b, k_b, v_b, a_lr, z_b, g_a, g_b) = jnp.split(z, offs, axis=-1)
    q_a = rotary(q_a.reshape(B_, T, ATTN_HEADS, HEAD_DIM), pos)
    k_a = rotary(k_a.reshape(B_, T, ATTN_HEADS, HEAD_DIM), pos)
    v_a = v_a.reshape(B_, T, ATTN_HEADS, HEAD_DIM)
    q_i = rotary(q_i.reshape(B_, T, IDX_HEADS, IDX_DIM), pos)
    k_i = rotary(rms_norm(k_i, p['kidx_norm_g'])[:, :, None, :], pos)[:, :, 0, :]
    w_i = w_i * (IDX_HEADS ** -0.5 * IDX_DIM ** -0.5)
    q_b = q_b.reshape(B_, T, GLA_HEADS, GLA_DK_HEAD) * (GLA_DK_HEAD ** -0.5)
    k_b = k_b.reshape(B_, T, GLA_HEADS, GLA_DK_HEAD)
    v_b = v_b.reshape(B_, T, GLA_HEADS, GLA_DV_HEAD)
    pre_gate = jnp.einsum('btr,rk->btk', a_lr, p['w_alpha2']) + p['b_alpha']
    log_a = (jax.nn.log_sigmoid(pre_gate.astype(jnp.float32)) / GATE_TEMP).reshape(B_, T, GLA_HEADS, GLA_DK_HEAD)
    return {'q_a': q_a, 'k_a': k_a, 'v_a': v_a, 'z_a': z_a, 'q_i': q_i, 'k_i': k_i, 'w_i': w_i,
            'q_b': q_b, 'k_b': k_b, 'v_b': v_b, 'log_a': log_a, 'z_b': z_b, 'g_a': g_a, 'g_b': g_b}


def indexer_topk(q_i, k_i, w_i, q_pos, top):
    L = k_i.shape[1]
    logits = jnp.einsum('bqhd,bld->bhql', q_i.astype(jnp.float32), k_i.astype(jnp.float32))
    score = jnp.einsum('bqh,bhql->bql', w_i.astype(jnp.float32), jax.nn.relu(logits))
    causal = jnp.arange(L, dtype=jnp.int32)[None, :] <= q_pos[:, None]
    score = jnp.where(causal[None], score, NEG_INF)
    _, idx = lax.top_k(score, top)
    valid = idx <= q_pos[None, :, None]
    return idx, valid


def sparse_attend(q, k_sel, v_sel, valid):
    s = jnp.einsum('bqhd,bqkhd->bhqk', q.astype(jnp.float32), k_sel.astype(jnp.float32)) * (HEAD_DIM ** -0.5)
    s = jnp.where(valid[:, None], s, NEG_INF)
    pr = jax.nn.softmax(s, axis=-1)
    return jnp.einsum('bhqk,bqkhd->bqhd', pr, v_sel.astype(jnp.float32)).astype(q.dtype)


def prompt_sparse_attention(pr):
    q_a, k_a, v_a, q_i, k_i, w_i = pr['q_a'], pr['k_a'], pr['v_a'], pr['q_i'], pr['k_i'], pr['w_i']
    B_, S = q_a.shape[:2]
    top = min(TOPK_MAX, S // 4)

    def block(i):
        start = i * Q_BLOCK
        sl = lambda a: lax.dynamic_slice_in_dim(a, start, Q_BLOCK, axis=1)
        q_pos = start + jnp.arange(Q_BLOCK, dtype=jnp.int32)
        idx, valid = indexer_topk(sl(q_i), k_i, sl(w_i), q_pos, top)
        return sparse_attend(sl(q_a), gather_rows(k_a, idx), gather_rows(v_a, idx), valid)

    out = lax.map(block, jnp.arange(S // Q_BLOCK, dtype=jnp.int32))
    return out.transpose(1, 0, 2, 3, 4).reshape(B_, S, ATTN_HEADS, HEAD_DIM)


def sample_sparse_attention(pr, cache_k, cache_v, cache_kidx, page_table, q_pos):
    q_a, k_a, v_a, q_i, k_i, w_i = pr['q_a'], pr['k_a'], pr['v_a'], pr['q_i'], pr['k_i'], pr['w_i']
    DB, T = q_a.shape[:2]
    n_pages = page_table.shape[1]
    past = n_pages * PAGE_SIZE
    top = min(TOPK_MAX, (past + T) // 4)
    kidx_past = cache_kidx[page_table].reshape(DB, past, IDX_DIM)
    kidx_all = jnp.concatenate([kidx_past, k_i.astype(kidx_past.dtype)], axis=1)
    idx, valid = indexer_topk(q_i, kidx_all, w_i, q_pos, top)
    in_past = (idx < past)[..., None, None]
    page = jnp.minimum(idx // PAGE_SIZE, n_pages - 1)
    off = idx % PAGE_SIZE
    phys = gather_rows(page_table, page)
    new_j = jnp.clip(idx - past, 0, T - 1)
    k_sel = jnp.where(in_past, cache_k[phys, off], gather_rows(k_a, new_j).astype(cache_k.dtype))
    v_sel = jnp.where(in_past, cache_v[phys, off], gather_rows(v_a, new_j).astype(cache_v.dtype))
    return sparse_attend(q_a, k_sel, v_sel, valid)


def gla_chunked(q, k, v, log_a):
    B_, T, H, DK = q.shape
    DV = v.shape[-1]
    C = GLA_CHUNK
    n = T // C

    def chunks(a):
        return a.astype(jnp.float32).reshape(B_, n, C, H, a.shape[-1]).transpose(1, 0, 3, 2, 4)

    qc, kc, vc, gc = chunks(q), chunks(k), chunks(v), chunks(log_a)
    b = jnp.cumsum(gc, axis=-2)
    b_last = b[..., -1:, :]
    q_dec = qc * jnp.exp(b)
    k_dec = kc * jnp.exp(-b)
    causal = jnp.tril(jnp.ones((C, C), dtype=bool))
    attn = jnp.where(causal, jnp.einsum('nbhcd,nbhsd->nbhcs', q_dec, k_dec), 0.0)
    o_intra = jnp.einsum('nbhcs,nbhsv->nbhcv', attn, vc)
    k_state = kc * jnp.exp(b_last - b)
    decay = jnp.exp(b_last[..., 0, :])

    def step(S, inp):
        qd, kst, vv, dec = inp
        o_inter = jnp.einsum('bhcd,bhdv->bhcv', qd, S)
        S_new = dec[..., None] * S + jnp.einsum('bhcd,bhcv->bhdv', kst, vv)
        return S_new, o_inter

    S0 = jnp.zeros((B_, H, DK, DV), jnp.float32)
    S_final, o_inter = lax.scan(step, S0, (q_dec, k_state, vc, decay))
    o = (o_intra + o_inter).transpose(1, 0, 3, 2, 4).reshape(B_, T, H, DV)
    return o, S_final


def gla_recurrent(q, k, v, log_a, S0):
    def step(S, inp):
        qt, kt, vt, at = inp
        S = jnp.exp(at)[..., None] * S + kt[..., :, None] * vt[..., None, :]
        return S, jnp.einsum('bhd,bhdv->bhv', qt, S)

    xs = tuple(a.astype(jnp.float32).transpose(1, 0, 2, 3) for a in (q, k, v, log_a))
    S, o = lax.scan(step, S0.astype(jnp.float32), xs)
    return o.transpose(1, 0, 2, 3), S


def merge(x, o_attn, o_gla, pr, p):
    B_, T, _ = x.shape
    o_attn = o_attn.reshape(B_, T, D_ATTN) * jax.nn.silu(pr['z_a'])
    o_gla = rms_norm(o_gla.astype(x.dtype), p['gla_norm_g'].reshape(GLA_HEADS, GLA_DV_HEAD))
    o_gla = o_gla.reshape(B_, T, GLA_DV) * jax.nn.silu(pr['z_b'])
    branch_a = jnp.einsum('bte,ed->btd', o_attn, p['w_o_attn'])
    branch_b = jnp.einsum('bte,ed->btd', o_gla, p['w_o_gla'])
    mixed = jax.nn.sigmoid(pr['g_a']) * branch_a + jax.nn.sigmoid(pr['g_b']) * branch_b
    y = jnp.einsum('btd,de->bte', mixed, p['w_out'])
    return x + rms_norm(y, p['post_norm_g'])


def setup_inputs(seed: int = 0) -> dict:
    key = jax.random.key(seed)
    ks = jax.random.split(key, 18)
    f32 = jnp.float32
    n_pages = PAST_LEN // PAGE_SIZE
    n_used = DEC_BATCH * n_pages
    n_pool = n_used + n_used // 4

    def nrm(k, shape, scale):
        return jax.random.normal(k, shape, f32) * scale

    x_prompt = nrm(ks[0], (BATCH, SEQ, D_MODEL), 1.0)
    x_sample = nrm(ks[1], (DEC_BATCH, DEC_SEQ, D_MODEL), 1.0)
    cache_k = nrm(ks[2], (DEPTH, n_pool, PAGE_SIZE, ATTN_HEADS, HEAD_DIM), 1.0)
    cache_v = nrm(ks[3], (DEPTH, n_pool, PAGE_SIZE, ATTN_HEADS, HEAD_DIM), 1.0)
    cache_kidx = nrm(ks[4], (DEPTH, n_pool, PAGE_SIZE, IDX_DIM), 1.0)
    state_gla = nrm(ks[5], (DEPTH, DEC_BATCH, GLA_HEADS, GLA_DK_HEAD, GLA_DV_HEAD), 1.0)
    page_table = jax.random.permutation(ks[6], n_pool)[:n_used].reshape(DEC_BATCH, n_pages).astype(jnp.int32)
    pre_norm_g = 1.0 + nrm(ks[7], (DEPTH, D_MODEL), 0.05)
    w_in = nrm(ks[8], (DEPTH, D_MODEL, IN_WIDTH), D_MODEL ** -0.5)
    kidx_norm_g = 1.0 + nrm(ks[9], (DEPTH, IDX_DIM), 0.05)
    w_alpha2 = nrm(ks[10], (DEPTH, GATE_RANK, GLA_DK), GATE_RANK ** -0.5)
    b_alpha = nrm(ks[11], (DEPTH, GLA_DK), 0.01)
    gla_norm_g = 1.0 + nrm(ks[12], (DEPTH, GLA_DV), 0.05)
    w_o_attn = nrm(ks[13], (DEPTH, D_ATTN, D_MODEL), D_ATTN ** -0.5)
    w_o_gla = nrm(ks[14], (DEPTH, GLA_DV, D_MODEL), GLA_DV ** -0.5)
    w_out = nrm(ks[15], (DEPTH, D_MODEL, D_MODEL), D_MODEL ** -0.5)
    post_norm_g = 1.0 + nrm(ks[16], (DEPTH, D_MODEL), 0.05)
    return {'x_prompt': x_prompt, 'x_sample': x_sample, 'cache_k': cache_k, 'cache_v': cache_v,
            'cache_kidx': cache_kidx, 'state_gla': state_gla, 'page_table': page_table,
            'pre_norm_g': pre_norm_g, 'w_in': w_in, 'kidx_norm_g': kidx_norm_g, 'w_alpha2': w_alpha2,
            'b_alpha': b_alpha, 'gla_norm_g': gla_norm_g, 'w_o_attn': w_o_attn, 'w_o_gla': w_o_gla,
            'w_out': w_out, 'post_norm_g': post_norm_g}


def reference(x_prompt, x_sample, cache_k, cache_v, cache_kidx, state_gla, page_table,
              pre_norm_g, w_in, kidx_norm_g, w_alpha2, b_alpha, gla_norm_g,
              w_o_attn, w_o_gla, w_out, post_norm_g):
    past_len = page_table.shape[1] * PAGE_SIZE
    pos_prompt = jnp.arange(x_prompt.shape[1], dtype=jnp.int32)
    pos_sample = past_len + jnp.arange(x_sample.shape[1], dtype=jnp.int32)
    xp, xs = x_prompt, x_sample
    k_p, v_p, ki_p, s_p = [], [], [], []
    k_s, v_s, ki_s, s_s = [], [], [], []
    for layer in range(DEPTH):
        p = {'pre_norm_g': pre_norm_g[layer], 'w_in': w_in[layer], 'kidx_norm_g': kidx_norm_g[layer],
             'w_alpha2': w_alpha2[layer], 'b_alpha': b_alpha[layer], 'gla_norm_g': gla_norm_g[layer],
             'w_o_attn': w_o_attn[layer], 'w_o_gla': w_o_gla[layer], 'w_out': w_out[layer],
             'post_norm_g': post_norm_g[layer]}
        pr = project(xp, pos_prompt, p)
        o_attn = prompt_sparse_attention(pr)
        o_gla, gla_final = gla_chunked(pr['q_b'], pr['k_b'], pr['v_b'], pr['log_a'])
        xp = merge(xp, o_attn, o_gla, pr, p)
        k_p.append(pr['k_a'])
        v_p.append(pr['v_a'])
        ki_p.append(pr['k_i'])
        s_p.append(gla_final)
        ps = project(xs, pos_sample, p)
        o_attn_s = sample_sparse_attention(ps, cache_k[layer], cache_v[layer], cache_kidx[layer], page_table, pos_sample)
        o_gla_s, gla_new = gla_recurrent(ps['q_b'], ps['k_b'], ps['v_b'], ps['log_a'], state_gla[layer])
        xs = merge(xs, o_attn_s, o_gla_s, ps, p)
        k_s.append(ps['k_a'])
        v_s.append(ps['v_a'])
        ki_s.append(ps['k_i'])
        s_s.append(gla_new)
    return (xp, xs, jnp.stack(k_p), jnp.stack(v_p), jnp.stack(ki_p), jnp.stack(s_p),
            jnp.stack(k_s), jnp.stack(v_s), jnp.stack(ki_s), jnp.stack(s_s))
```

```python
import functools

import numpy as np
import jax
import jax.numpy as jnp
from jax import lax
from jax.experimental import pallas as pl
from jax.experimental.pallas import tpu as pltpu

F32 = jnp.float32
BF16 = jnp.bfloat16
I32 = jnp.int32

HEAD_DIM = 128
IDX_HEADS = 4
IDX_DIM = 64
TOPK_MAX = 256
GLA_HEADS = 4
GATE_RANK = 16
GATE_TEMP = 16.0
GLA_CHUNK = 64
ROPE_THETA = 10000.0
NORM_EPS = 1e-6
NEG_INF = -1e30
PAGE_SIZE = 128

LANES = 128
INT_MIN = np.int32(-2147483648)
VMEM_LIMIT = 56 * 1024 * 1024

MISC_W = IDX_DIM
MISC_A = IDX_DIM + IDX_HEADS
IDX_GROUP = 384


def _nt(a, b):
    return lax.dot_general(a, b, (((1,), (1,)), ((), ())), preferred_element_type=F32)


def _rms(x, g):
    var = jnp.mean(x * x, axis=-1, keepdims=True)
    return x * lax.rsqrt(var + NORM_EPS) * g


def _params(sem):
    return pltpu.CompilerParams(dimension_semantics=sem, vmem_limit_bytes=VMEM_LIMIT)


def _proj_attn_kernel(x_ref, g_ref, w_ref, cos_ref, sin_ref,
                      q_ref, kf_ref, kb_ref, vf_ref, vb_ref, sza_ref, *, d_attn):
    h = _rms(x_ref[...], g_ref[...]).astype(BF16)
    cos = cos_ref[...]
    sin = sin_ref[...]
    nh = d_attn // HEAD_DIM

    def rope(z):
        outs = []
        for hd in range(nh):
            zh = z[:, hd * HEAD_DIM:(hd + 1) * HEAD_DIM]
            outs.append(zh * cos + pltpu.roll(zh, HEAD_DIM // 2, axis=1) * sin)
        return jnp.concatenate(outs, axis=1)

    zq = jnp.dot(h, w_ref[:, 0:d_attn], preferred_element_type=F32)
    q_ref[...] = (rope(zq) * (HEAD_DIM ** -0.5)).astype(BF16)
    zk = rope(jnp.dot(h, w_ref[:, d_attn:2 * d_attn], preferred_element_type=F32))
    kf_ref[...] = zk
    kb_ref[...] = zk.astype(BF16)
    zv = jnp.dot(h, w_ref[:, 2 * d_attn:3 * d_attn], preferred_element_type=F32)
    vf_ref[...] = zv
    vb_ref[...] = zv.astype(BF16)
    zz = jnp.dot(h, w_ref[:, 3 * d_attn:4 * d_attn], preferred_element_type=F32)
    sza_ref[...] = (zz * jax.nn.sigmoid(zz)).astype(BF16)


def _proj_attn(x, g, w_a, cos, sin, tm, pos_blocks):
    n, d = x.shape
    d_attn = w_a.shape[1] // 4
    row = lambda i: (i, 0)
    fixed = lambda i: (0, 0)
    posmap = (lambda i: (i % pos_blocks, 0))
    out_sds = [jax.ShapeDtypeStruct((n, d_attn), dt) for dt in (BF16, F32, BF16, F32, BF16, BF16)]
    return pl.pallas_call(
        functools.partial(_proj_attn_kernel, d_attn=d_attn),
        grid=(n // tm,),
        in_specs=[pl.BlockSpec((tm, d), row), pl.BlockSpec((1, d), fixed),
                  pl.BlockSpec(w_a.shape, fixed),
                  pl.BlockSpec((tm, LANES), posmap), pl.BlockSpec((tm, LANES), posmap)],
        out_specs=[pl.BlockSpec((tm, d_attn), row)] * 6,
        out_shape=out_sds,
        compiler_params=_params(("parallel",)),
        name="proj_attn",
    )(x, g, w_a, cos, sin)


def _proj_mix_kernel(x_ref, g_ref, wb_ref, w5_ref, kg_ref, w2_ref, ba_ref, cos_ref, sin_ref,
                     qb_ref, kb_ref, vb_ref, la_ref, szb_ref, sga_ref, sgb_ref,
                     qcat_ref, kcat_ref, misc_ref, *, dk, dv, dm):
    hf = _rms(x_ref[...], g_ref[...])
    h_hi = hf.astype(BF16)
    h_lo = (hf - h_hi.astype(F32)).astype(BF16)

    o = 0
    zq = jnp.dot(h_hi, wb_ref[:, o:o + dk], preferred_element_type=F32)
    qb_ref[...] = zq * ((dk // GLA_HEADS) ** -0.5)
    o += dk
    kb_ref[...] = jnp.dot(h_hi, wb_ref[:, o:o + dk], preferred_element_type=F32)
    o += dk
    vb_ref[...] = jnp.dot(h_hi, wb_ref[:, o:o + dv], preferred_element_type=F32)
    o += dv
    zz = jnp.dot(h_hi, wb_ref[:, o:o + dv], preferred_element_type=F32)
    szb_ref[...] = (zz * jax.nn.sigmoid(zz)).astype(BF16)
    o += dv
    sga_ref[...] = jax.nn.sigmoid(jnp.dot(h_hi, wb_ref[:, o:o + dm], preferred_element_type=F32)).astype(BF16)
    o += dm
    sgb_ref[...] = jax.nn.sigmoid(jnp.dot(h_hi, wb_ref[:, o:o + dm], preferred_element_type=F32)).astype(BF16)

    hcat = jnp.concatenate([h_hi, h_hi, h_lo], axis=1)
    z5 = jnp.dot(hcat, w5_ref[...], preferred_element_type=F32)
    cos = cos_ref[...]
    sin = sin_ref[...]
    lane = lax.broadcasted_iota(I32, (1, LANES), 1)
    first_half = (lane & (IDX_DIM - 1)) < (IDX_DIM // 2)
    low64 = lane < IDX_DIM

    def rope64(z):
        rot = jnp.where(first_half, pltpu.roll(z, LANES - IDX_DIM // 2, axis=1),
                        pltpu.roll(z, IDX_DIM // 2, axis=1))
        return z * cos + rot * sin

    def hilo(z):
        hi = z.astype(BF16).astype(F32)
        lo = (z - hi).astype(BF16).astype(F32)
        return hi, lo

    pieces = []
    for p in range(IDX_HEADS * IDX_DIM // LANES):
        zr = rope64(z5[:, p * LANES:(p + 1) * LANES])
        sw = pltpu.roll(zr, IDX_DIM, axis=1)
        for dup in (jnp.where(low64, zr, sw), jnp.where(low64, sw, zr)):
            hi, lo = hilo(dup)
            pieces += [hi.astype(BF16), lo.astype(BF16)]
    qcat_ref[...] = jnp.concatenate(pieces, axis=1)

    m = z5[:, IDX_GROUP - LANES:IDX_GROUP]
    ki = jnp.where(low64, m, 0.0)
    var = jnp.sum(ki * ki, axis=-1, keepdims=True) * (1.0 / IDX_DIM)
    kin = rope64(ki * lax.rsqrt(var + NORM_EPS) * kg_ref[...])
    is_w = (lane >= MISC_W) & (lane < MISC_A)
    misc = jnp.where(low64, kin, jnp.where(is_w, m * (IDX_HEADS ** -0.5 * IDX_DIM ** -0.5), m))
    misc_ref[...] = misc
    kdup = jnp.where(low64, kin, pltpu.roll(kin, IDX_DIM, axis=1))
    hi, lo = hilo(kdup)
    kpiece = jnp.where(low64, hi, lo).astype(BF16)
    kcat_ref[...] = jnp.concatenate([kpiece, kpiece], axis=1)

    pre = jnp.dot(m, w2_ref[...], preferred_element_type=F32, precision=lax.Precision.HIGHEST) + ba_ref[...]
    la_ref[...] = jax.nn.log_sigmoid(pre) * (1.0 / GATE_TEMP)


def _proj_mix(x, g, w_b, w5, kg, w2, ba, cos, sin, tm, pos_blocks, dk, dv, dm):
    n, d = x.shape
    row = lambda i: (i, 0)
    fixed = lambda i: (0, 0)
    posmap = (lambda i: (i % pos_blocks, 0))
    widths = [(dk, F32), (dk, F32), (dv, F32), (dk, F32), (dv, BF16), (dm, BF16), (dm, BF16),
              (IDX_HEADS * 4 * IDX_DIM, BF16), (4 * IDX_DIM, BF16), (LANES, F32)]
    return pl.pallas_call(
        functools.partial(_proj_mix_kernel, dk=dk, dv=dv, dm=dm),
        grid=(n // tm,),
        in_specs=[pl.BlockSpec((tm, d), row), pl.BlockSpec((1, d), fixed),
                  pl.BlockSpec(w_b.shape, fixed), pl.BlockSpec(w5.shape, fixed),
                  pl.BlockSpec((1, LANES), fixed), pl.BlockSpec(w2.shape, fixed),
                  pl.BlockSpec((1, dk), fixed),
                  pl.BlockSpec((tm, LANES), posmap), pl.BlockSpec((tm, LANES), posmap)],
        out_specs=[pl.BlockSpec((tm, w), row) for w, _ in widths],
        out_shape=[jax.ShapeDtypeStruct((n, w), dt) for w, dt in widths],
        compiler_params=_params(("parallel",)),
        name="proj_mix",
    )(x, g, w_b, w5, kg, w2, ba, cos, sin)


def _sort_key(score):
    bits = lax.bitcast_convert_type(score, I32)
    key = jnp.where(bits < 0, bits ^ np.int32(0x7FFFFFFF), bits)
    return jnp.where(bits == INT_MIN, np.int32(0), key)


def _kth_largest_key(count_ge, shape, top):
    def body(it, tu):
        cu = tu | lax.shift_left(np.int32(1), np.int32(31) - it)
        cnt = count_ge(cu ^ INT_MIN)
        return jnp.where(cnt >= top, cu, tu)
    tu = lax.fori_loop(0, 32, body, jnp.zeros(shape, I32))
    return tu ^ INT_MIN


def _prompt_attn_kernel(qcat_ref, misc_ref, kcat_ref, q_ref, k_ref, v_ref, o_ref,
                        keys_ref, acc_ref, m_ref, l_ref, *, tq, kblk, top, nh):
    i = pl.program_id(1)
    nkb = ((i + 1) * tq + kblk - 1) // kblk
    q_pos = i * tq + lax.broadcasted_iota(I32, (tq, 1), 0)
    lane_k = lax.broadcasted_iota(I32, (1, kblk), 1)
    nlc = kblk // LANES

    def score_body(kb, c):
        k0 = pl.multiple_of(kb * kblk, kblk)
        kc = kcat_ref[pl.ds(k0, kblk), :]
        sc = jnp.zeros((tq, kblk), F32)
        for h in range(IDX_HEADS):
            lg = _nt(qcat_ref[:, h * 4 * IDX_DIM:(h + 1) * 4 * IDX_DIM], kc)
            sc = sc + misc_ref[:, MISC_W + h:MISC_W + h + 1] * jnp.maximum(lg, 0.0)
        key = jnp.where(k0 + lane_k <= q_pos, _sort_key(sc), INT_MIN)
        keys_ref[kb] = key
        return c

    lax.fori_loop(0, nkb, score_body, 0)

    def count(pred):
        def body(kb, acc):
            m = pred(keys_ref[kb])
            for j in range(nlc):
                acc = acc + jnp.where(m[:, j * LANES:(j + 1) * LANES], 1, 0)
            return acc
        acc = lax.fori_loop(0, nkb, body, jnp.zeros((tq, LANES), I32))
        return jnp.sum(acc, axis=1, keepdims=True)

    thr = _kth_largest_key(lambda c: count(lambda kk: kk >= c), (tq, 1), top)
    need = (top - count(lambda kk: kk > thr)).astype(F32)

    tri = jnp.where(lax.broadcasted_iota(I32, (kblk, kblk), 0) <= lax.broadcasted_iota(I32, (kblk, kblk), 1),
                    1.0, 0.0).astype(BF16)
    m_ref[...] = jnp.full(m_ref.shape, NEG_INF, F32)
    l_ref[...] = jnp.zeros(l_ref.shape, F32)
    acc_ref[...] = jnp.zeros(acc_ref.shape, F32)

    def attn_body(kb, eqc):
        k0 = pl.multiple_of(kb * kblk, kblk)
        kk = keys_ref[kb]
        causal = k0 + lane_k <= q_pos
        eq = (kk == thr) & causal
        pre = jnp.dot(jnp.where(eq, 1.0, 0.0).astype(BF16), tri, preferred_element_type=F32) + eqc
        sel = (kk > thr) | (eq & (pre <= need))
        kblock = k_ref[pl.ds(k0, kblk), :]
        vblock = v_ref[pl.ds(k0, kblk), :]
        for h in range(nh):
            hs = slice(h * HEAD_DIM, (h + 1) * HEAD_DIM)
            s = jnp.where(sel, _nt(q_ref[:, hs], kblock[:, hs]), NEG_INF)
            m_old = m_ref[h]
            m_new = jnp.maximum(m_old, jnp.max(s, axis=1, keepdims=True))
            alpha = jnp.exp(m_old - m_new)
            p = jnp.exp(s - m_new)
            l_ref[h] = alpha * l_ref[h] + jnp.sum(p, axis=1, keepdims=True)
            acc_ref[:, hs] = alpha * acc_ref[:, hs] + jnp.dot(p.astype(BF16), vblock[:, hs],
                                                              preferred_element_type=F32)
            m_ref[h] = m_new
        return pre[:, kblk - 1:kblk]

    lax.fori_loop(0, nkb, attn_body, jnp.zeros((tq, 1), F32))
    for h in range(nh):
        hs = slice(h * HEAD_DIM, (h + 1) * HEAD_DIM)
        o_ref[:, hs] = (acc_ref[:, hs] / l_ref[h]).astype(o_ref.dtype)


def _prompt_attn(qcat, misc, kcat, q, k, v, batch, seq, tq, kblk, top):
    n, d_attn = q.shape
    nh = d_attn // HEAD_DIM
    nq = seq // tq
    tile = lambda b, i: (b * nq + i, 0)
    whole = lambda b, i: (b, 0)
    return pl.pallas_call(
        functools.partial(_prompt_attn_kernel, tq=tq, kblk=kblk, top=top, nh=nh),
        grid=(batch, nq),
        in_specs=[pl.BlockSpec((tq, qcat.shape[1]), tile), pl.BlockSpec((tq, LANES), tile),
                  pl.BlockSpec((seq, kcat.shape[1]), whole),
                  pl.BlockSpec((tq, d_attn), tile),
                  pl.BlockSpec((seq, d_attn), whole), pl.BlockSpec((seq, d_attn), whole)],
        out_specs=pl.BlockSpec((tq, d_attn), tile),
        out_shape=jax.ShapeDtypeStruct((n, d_attn), BF16),
        scratch_shapes=[pltpu.VMEM((seq // kblk, tq, kblk), I32),
                        pltpu.VMEM((tq, d_attn), F32),
                        pltpu.VMEM((nh, tq, 1), F32), pltpu.VMEM((nh, tq, 1), F32)],
        compiler_params=_params(("parallel", "arbitrary")),
        name="prompt_attn",
    )(qcat, misc, kcat, q, k, v)


def _col(row):
    n = row.shape[1]
    eye = lax.broadcasted_iota(I32, (n, n), 0) == lax.broadcasted_iota(I32, (n, n), 1)
    return jnp.sum(jnp.where(eye, row, 0.0), axis=1, keepdims=True)


def _gla_chunk_kernel(q_ref, k_ref, v_ref, la_ref, o_ref, sfin_ref, s_ref, *, tt, chunk):
    t = pl.program_id(2)

    @pl.when(t == 0)
    def _():
        s_ref[...] = jnp.zeros(s_ref.shape, F32)

    r = lax.broadcasted_iota(I32, (tt, tt), 0)
    c = lax.broadcasted_iota(I32, (tt, tt), 1)
    sh = int(np.log2(chunk))
    cum = jnp.where((c <= r) & (jnp.right_shift(c, sh) == jnp.right_shift(r, sh)), 1.0, 0.0)
    b = jnp.dot(cum, la_ref[...], preferred_element_type=F32, precision=lax.Precision.HIGHEST)
    causal = (lax.broadcasted_iota(I32, (chunk, chunk), 1) <= lax.broadcasted_iota(I32, (chunk, chunk), 0))
    s = s_ref[...]
    for ci in range(tt // chunk):
        rs = slice(ci * chunk, (ci + 1) * chunk)
        bc = b[rs]
        bl = bc[chunk - 1:chunk]
        qd = (q_ref[rs, :] * jnp.exp(bc)).astype(BF16)
        kc = k_ref[rs, :]
        kd = (kc * jnp.exp(-bc)).astype(BF16)
        ks = (kc * jnp.exp(bl - bc)).astype(BF16)
        vc = v_ref[rs, :].astype(BF16)
        attn = jnp.where(causal, _nt(qd, kd), 0.0).astype(BF16)
        o_ref[rs, :] = (jnp.dot(attn, vc, preferred_element_type=F32)
                        + jnp.dot(qd, s.astype(BF16), preferred_element_type=F32))
        kv = lax.dot_general(ks, vc, (((0,), (0,)), ((), ())), preferred_element_type=F32)
        s = jnp.exp(_col(bl)) * s + kv
    s_ref[...] = s
    sfin_ref[0, 0] = s


def _gla_chunked(qb, kb, vb, la, batch, seq, tt):
    n, dk = qb.shape
    dv = vb.shape[1]
    dkh, dvh = dk // GLA_HEADS, dv // GLA_HEADS
    nt = seq // tt
    tile = lambda b, h, t: (b * nt + t, h)
    return pl.pallas_call(
        functools.partial(_gla_chunk_kernel, tt=tt, chunk=GLA_CHUNK),
        grid=(batch, GLA_HEADS, nt),
        in_specs=[pl.BlockSpec((tt, dkh), tile), pl.BlockSpec((tt, dkh), tile),
                  pl.BlockSpec((tt, dvh), tile), pl.BlockSpec((tt, dkh), tile)],
        out_specs=[pl.BlockSpec((tt, dvh), tile),
                   pl.BlockSpec((1, 1, dkh, dvh), lambda b, h, t: (b, h, 0, 0))],
        out_shape=[jax.ShapeDtypeStruct((n, dv), F32),
                   jax.ShapeDtypeStruct((batch, GLA_HEADS, dkh, dvh), F32)],
        scratch_shapes=[pltpu.VMEM((dkh, dvh), F32)],
        compiler_params=_params(("parallel", "parallel", "arbitrary")),
        name="gla_chunked",
    )(qb, kb, vb, la)


def _gla_step_kernel(q_ref, k_ref, v_ref, la_ref, s0_ref, o_ref, s_ref, *, dkh, dvh):
    for h in range(GLA_HEADS):
        ks = slice(h * dkh, (h + 1) * dkh)
        vs = slice(h * dvh, (h + 1) * dvh)
        s = jnp.exp(_col(la_ref[0, :, ks])) * s0_ref[0, h] + _col(k_ref[0, :, ks]) * v_ref[0, :, vs]
        s_ref[0, h] = s
        o_ref[0, :, vs] = jnp.sum(_col(q_ref[0, :, ks]) * s, axis=0, keepdims=True)


def _gla_step(qb, kb, vb, la, s0):
    db, dk = qb.shape
    dv = vb.shape[1]
    dkh, dvh = dk // GLA_HEADS, dv // GLA_HEADS
    vec = lambda w: pl.BlockSpec((1, 1, w), lambda b: (b, 0, 0))
    st = pl.BlockSpec((1, GLA_HEADS, dkh, dvh), lambda b: (b, 0, 0, 0))
    r3 = lambda a: a.reshape(db, 1, a.shape[1])
    o, s = pl.pallas_call(
        functools.partial(_gla_step_kernel, dkh=dkh, dvh=dvh),
        grid=(db,),
        in_specs=[vec(dk), vec(dk), vec(dv), vec(dk), st],
        out_specs=[vec(dv), st],
        out_shape=[jax.ShapeDtypeStruct((db, 1, dv), F32), jax.ShapeDtypeStruct(s0.shape, F32)],
        compiler_params=_params(("parallel",)),
        name="gla_step",
    )(r3(qb), r3(kb), r3(vb), r3(la), s0)
    return o.reshape(db, dv), s


def _merge_kernel(x_ref, oa_ref, sza_ref, og_ref, szb_ref, sga_ref, sgb_ref,
                  gn_ref, woa_ref, wog_ref, wout_ref, pn_ref, y_ref, *, dvh):
    a_in = oa_ref[...] * sza_ref[...]
    br_a = jnp.dot(a_in, woa_ref[...], preferred_element_type=F32)
    og = og_ref[...]
    gn = gn_ref[...]
    parts = []
    for h in range(GLA_HEADS):
        vs = slice(h * dvh, (h + 1) * dvh)
        parts.append(_rms(og[:, vs], gn[:, vs]))
    b_in = (jnp.concatenate(parts, axis=1) * szb_ref[...].astype(F32)).astype(BF16)
    br_b = jnp.dot(b_in, wog_ref[...], preferred_element_type=F32)
    mixed = sga_ref[...].astype(F32) * br_a + sgb_ref[...].astype(F32) * br_b
    y = jnp.dot(mixed.astype(BF16), wout_ref[...], preferred_element_type=F32)
    y_ref[...] = x_ref[...] + _rms(y, pn_ref[...])


def _merge(x, oa, sza, og, szb, sga, sgb, gn, woa, wog, wout, pn, tm):
    n, d = x.shape
    row = lambda w: pl.BlockSpec((tm, w), lambda i: (i, 0))
    fixed = lambda a: pl.BlockSpec(a.shape, lambda i: (0, 0))
    return pl.pallas_call(
        functools.partial(_merge_kernel, dvh=og.shape[1] // GLA_HEADS),
        grid=(n // tm,),
        in_specs=[row(d), row(oa.shape[1]), row(sza.shape[1]), row(og.shape[1]), row(szb.shape[1]),
                  row(d), row(d), fixed(gn), fixed(woa), fixed(wog), fixed(wout), fixed(pn)],
        out_specs=row(d),
        out_shape=jax.ShapeDtypeStruct((n, d), F32),
        compiler_params=_params(("parallel",)),
        name="merge",
    )(x, oa, sza, og, szb, sga, sgb, gn, woa, wog, wout, pn)


def _sample_score_kernel(pt_ref, q8_ref, misc_ref, kidx_hbm, sc_ref, self_ref, kbuf, sem,
                         *, n_pages, chunk_pages):
    b = pl.program_id(0)
    nb = pl.num_programs(0)

    def page_copy(bb, p, slot):
        return pltpu.make_async_copy(kidx_hbm.at[pt_ref[bb, p]], kbuf.at[slot, p], sem.at[slot])

    def issue(bb, slot):
        def body(p, c):
            page_copy(bb, p, slot).start()
            return c
        lax.fori_loop(0, n_pages, body, 0)

    slot = b % 2

    @pl.when(b == 0)
    def _():
        issue(b, slot)

    @pl.when(b + 1 < nb)
    def _():
        issue(b + 1, 1 - slot)

    def wait_body(p, c):
        page_copy(b, p, slot).wait()
        return c
    lax.fori_loop(0, n_pages, wait_body, 0)

    q8 = q8_ref[0]
    wrow = misc_ref[0]

    def score_of(kf):
        hi = kf.astype(BF16)
        lo = (kf - hi.astype(F32)).astype(BF16)
        lg = _nt(q8, hi) + _nt(q8, lo)
        sc = jnp.zeros((1, kf.shape[0]), F32)
        for h in range(IDX_HEADS):
            lh = lg[h:h + 1] + lg[IDX_HEADS + h:IDX_HEADS + h + 1]
            sc = sc + wrow[:, MISC_W + h:MISC_W + h + 1] * jnp.maximum(lh, 0.0)
        return sc

    ck = chunk_pages * PAGE_SIZE
    for c in range(n_pages // chunk_pages):
        kf = kbuf[slot, c * chunk_pages:(c + 1) * chunk_pages].reshape(ck, IDX_DIM)
        sc_ref[0, :, c * ck:(c + 1) * ck] = score_of(kf)
    kself = jnp.broadcast_to(wrow[:, 0:IDX_DIM], (8, IDX_DIM))
    self_ref[0] = jnp.broadcast_to(score_of(kself)[:, 0:1], (1, LANES))


def _sample_scores(page_table, q8, misc, cache_kidx):
    db, n_pages = page_table.shape
    past = n_pages * PAGE_SIZE
    chunk_pages = 16 if n_pages % 16 == 0 else n_pages
    grid_spec = pltpu.PrefetchScalarGridSpec(
        num_scalar_prefetch=1,
        grid=(db,),
        in_specs=[pl.BlockSpec((1, 8, IDX_DIM), lambda b, pt: (b, 0, 0)),
                  pl.BlockSpec((1, 1, LANES), lambda b, pt: (b, 0, 0)),
                  pl.BlockSpec(memory_space=pl.ANY)],
        out_specs=[pl.BlockSpec((1, 1, past), lambda b, pt: (b, 0, 0)),
                   pl.BlockSpec((1, 1, LANES), lambda b, pt: (b, 0, 0))],
        scratch_shapes=[pltpu.VMEM((2, n_pages, PAGE_SIZE, IDX_DIM), F32),
                        pltpu.SemaphoreType.DMA((2,))],
    )
    return pl.pallas_call(
        functools.partial(_sample_score_kernel, n_pages=n_pages, chunk_pages=chunk_pages),
        grid_spec=grid_spec,
        out_shape=[jax.ShapeDtypeStruct((db, 1, past), F32), jax.ShapeDtypeStruct((db, 1, LANES), F32)],
        compiler_params=_params(("arbitrary",)),
        name="sample_scores",
    )(page_table, q8, misc.reshape(db, 1, LANES), cache_kidx)


def _sample_select_kernel(sc_ref, self_ref, idx_ref, keys_ref, thr_ref, kself_ref, *, top, past):
    db, rows, cols = sc_ref.shape
    keys_ref[...] = _sort_key(sc_ref[...])
    kself = _sort_key(self_ref[...][:, :, 0:1])
    kself_ref[...] = kself

    def count_ge(c):
        m = jnp.where(keys_ref[...] >= c, 1, 0)
        tot = jnp.sum(jnp.sum(m, axis=1, keepdims=True), axis=2, keepdims=True)
        return tot + jnp.where(kself >= c, 1, 0)

    thr_ref[...] = _kth_largest_key(count_ge, (db, 1, 1), top)

    def upper(n):
        return jnp.where(lax.broadcasted_iota(I32, (n, n), 0) <= lax.broadcasted_iota(I32, (n, n), 1),
                         1.0, 0.0).astype(BF16)
    tri_c = upper(cols)
    tri_r = upper(rows)
    low = jnp.where(lax.broadcasted_iota(I32, (rows, rows), 1) < lax.broadcasted_iota(I32, (rows, rows), 0),
                    1.0, 0.0).astype(BF16)
    slot = lax.broadcasted_iota(I32, (top, 1), 0).astype(F32)
    row_of = lax.broadcasted_iota(I32, (1, rows), 1).astype(F32)
    ones8 = jnp.ones((8, cols), BF16)

    def prefix(mask_bf):
        within = jnp.dot(mask_bf, tri_c, preferred_element_type=F32)
        before = jnp.sum(jnp.dot(low, mask_bf, preferred_element_type=F32), axis=1, keepdims=True)
        return within + before

    def body(b, c):
        kk = keys_ref[b]
        thr = thr_ref[b]
        gt = kk > thr
        eq = kk == thr
        n_gt = jnp.sum(jnp.sum(jnp.where(gt, 1.0, 0.0), axis=1, keepdims=True), axis=0, keepdims=True)
        n_gt = n_gt + jnp.where(kself_ref[b] > thr, 1.0, 0.0)
        need = top - n_gt
        eq_bf = jnp.where(eq, 1.0, 0.0).astype(BF16)
        sel = gt | (eq & (prefix(eq_bf) <= need))
        sel_bf = jnp.where(sel, 1.0, 0.0).astype(BF16)
        g = prefix(sel_bf)
        n_sel = g[rows - 1:rows, cols - 1:cols]
        row_tot = _nt(ones8, sel_bf)[0:1]
        row_cum = jnp.dot(row_tot.astype(BF16), tri_r, preferred_element_type=F32)
        page = jnp.sum(jnp.where(row_cum <= slot, 1.0, 0.0), axis=1, keepdims=True)
        onehot = jnp.where(row_of == page, 1.0, 0.0).astype(BF16)
        g_row = jnp.dot(onehot, g.astype(BF16), preferred_element_type=F32)
        off = jnp.sum(jnp.where(g_row <= slot, 1.0, 0.0), axis=1, keepdims=True)
        idx = page * float(cols) + off
        idx = jnp.where(slot < n_sel, idx, float(past))
        idx_ref[b] = jnp.broadcast_to(idx, (top, LANES)).T[0:8].astype(I32)
        return c

    lax.fori_loop(0, db, body, 0)


def _sample_select(scores, self_sc, top):
    db, _, past = scores.shape
    rows = past // LANES
    sc3 = scores.reshape(db, rows, LANES)
    out = pl.pallas_call(
        functools.partial(_sample_select_kernel, top=top, past=past),
        in_specs=[pl.BlockSpec(sc3.shape, lambda: (0, 0, 0)), pl.BlockSpec(self_sc.shape, lambda: (0, 0, 0))],
        out_specs=pl.BlockSpec((db, 8, top), lambda: (0, 0, 0)),
        out_shape=jax.ShapeDtypeStruct((db, 8, top), I32),
        scratch_shapes=[pltpu.VMEM(sc3.shape, I32), pltpu.VMEM((db, 1, 1), I32), pltpu.VMEM((db, 1, 1), I32)],
        compiler_params=pltpu.CompilerParams(vmem_limit_bytes=VMEM_LIMIT),
        name="sample_select",
    )(sc3, self_sc)
    return out[:, 0, :]


def _sample_attn_kernel(idx_ref, pt_ref, q_ref, knew_hbm, vnew_hbm, ck_hbm, cv_hbm, o_ref,
                        kbuf, vbuf, sem, *, top, past, n_pages):
    b = pl.program_id(0)
    nb = pl.num_programs(0)

    def issue(bb, slot):
        def body(j, c):
            ix = idx_ref[bb, j]

            @pl.when(ix < past)
            def _():
                phys = pt_ref[bb, jnp.minimum(ix // PAGE_SIZE, n_pages - 1)]
                off = ix % PAGE_SIZE
                pltpu.make_async_copy(ck_hbm.at[phys, off], kbuf.at[slot, j], sem.at[slot]).start()
                pltpu.make_async_copy(cv_hbm.at[phys, off], vbuf.at[slot, j], sem.at[slot]).start()

            @pl.when(ix >= past)
            def _():
                pltpu.make_async_copy(knew_hbm.at[bb], kbuf.at[slot, j], sem.at[slot]).start()
                pltpu.make_async_copy(vnew_hbm.at[bb], vbuf.at[slot, j], sem.at[slot]).start()
            return c
        lax.fori_loop(0, top, body, 0)

    slot = b % 2

    @pl.when(b == 0)
    def _():
        issue(b, slot)

    @pl.when(b + 1 < nb)
    def _():
        issue(b + 1, 1 - slot)

    def wait_body(j, c):
        pltpu.make_async_copy(knew_hbm.at[0], kbuf.at[slot, j], sem.at[slot]).wait()
        pltpu.make_async_copy(vnew_hbm.at[0], vbuf.at[slot, j], sem.at[slot]).wait()
        return c
    lax.fori_loop(0, top, wait_body, 0)

    q = q_ref[0]
    s = jnp.sum(kbuf[slot] * q[None], axis=2, keepdims=True)
    m = jnp.max(s, axis=0, keepdims=True)
    p = jnp.exp(s - m)
    l = jnp.sum(p, axis=0, keepdims=True)
    o = jnp.sum(p * vbuf[slot], axis=0)
    o_ref[0] = o / l[0]


def _sample_attn(idx, page_table, q, k_new, v_new, cache_k, cache_v, top):
    db, nh, hd = q.shape
    n_pages = page_table.shape[1]
    past = n_pages * PAGE_SIZE
    grid_spec = pltpu.PrefetchScalarGridSpec(
        num_scalar_prefetch=2,
        grid=(db,),
        in_specs=[pl.BlockSpec((1, nh, hd), lambda b, ix, pt: (b, 0, 0)),
                  pl.BlockSpec(memory_space=pl.ANY), pl.BlockSpec(memory_space=pl.ANY),
                  pl.BlockSpec(memory_space=pl.ANY), pl.BlockSpec(memory_space=pl.ANY)],
        out_specs=pl.BlockSpec((1, nh, hd), lambda b, ix, pt: (b, 0, 0)),
        scratch_shapes=[pltpu.VMEM((2, top, nh, hd), F32), pltpu.VMEM((2, top, nh, hd), F32),
                        pltpu.SemaphoreType.DMA((2,))],
    )
    return pl.pallas_call(
        functools.partial(_sample_attn_kernel, top=top, past=past, n_pages=n_pages),
        grid_spec=grid_spec,
        out_shape=jax.ShapeDtypeStruct((db, nh, hd), F32),
        compiler_params=_params(("arbitrary",)),
        name="sample_attn",
    )(idx, page_table, q, k_new, v_new, cache_k, cache_v)


def _rope_tables(pos, dim):
    half = dim // 2
    inv_freq = ROPE_THETA ** (-jnp.arange(half, dtype=F32) / half)
    ang = pos.astype(F32)[:, None] * inv_freq[None, :]
    cos, sin = jnp.cos(ang), jnp.sin(ang)
    rep = LANES // dim
    return (jnp.tile(jnp.concatenate([cos, cos], axis=1), (1, rep)),
            jnp.tile(jnp.concatenate([-sin, sin], axis=1), (1, rep)))


def _pick_tile(n, pref):
    t = min(n, pref)
    while n % t:
        t //= 2
    return t


def kernel(x_prompt, x_sample, cache_k, cache_v, cache_kidx, state_gla, page_table, pre_norm_g, w_in,
           kidx_norm_g, w_alpha2, b_alpha, gla_norm_g, w_o_attn, w_o_gla, w_out, post_norm_g):
    batch, seq, d = x_prompt.shape
    db, dseq, _ = x_sample.shape
    assert dseq == 1, "sample group is one token per sequence"
    depth = w_in.shape[0]
    n_pages = page_table.shape[1]
    past = n_pages * PAGE_SIZE
    d_attn = w_o_attn.shape[1]
    nh = d_attn // HEAD_DIM
    dk = w_alpha2.shape[2]
    dv = w_o_gla.shape[1]
    dm = d
    nqi = IDX_HEADS * IDX_DIM

    pos_p = jnp.arange(seq, dtype=jnp.int32)
    pos_s = jnp.full((db,), past, dtype=jnp.int32)
    cos_p, sin_p = _rope_tables(pos_p, HEAD_DIM)
    cosi_p, sini_p = _rope_tables(pos_p, IDX_DIM)
    cos_s, sin_s = _rope_tables(pos_s, HEAD_DIM)
    cosi_s, sini_s = _rope_tables(pos_s, IDX_DIM)

    tm = _pick_tile(seq, 256)
    tq = _pick_tile(seq, 256)
    kblk = _pick_tile(seq, 512)
    tt = _pick_tile(seq, 512)
    top_p = min(TOPK_MAX, seq // 4)
    top_s = min(TOPK_MAX, (past + 1) // 4)

    xp, xs = x_prompt.reshape(batch * seq, d), x_sample.reshape(db, d)
    outs = {k: [] for k in ("kp", "vp", "kip", "sp", "ks", "vs", "kis", "ss")}
    for layer in range(depth):
        w = w_in[layer]
        o_qi = 4 * d_attn
        o_ki = o_qi + nqi
        o_wi = o_ki + IDX_DIM
        o_qb = o_wi + IDX_HEADS
        o_kb = o_qb + dk
        o_vb = o_kb + dk
        o_al = o_vb + dv
        o_zb = o_al + GATE_RANK
        o_ga = o_zb + dv
        o_gb = o_ga + dm
        w_a = w[:, :o_qi].astype(BF16)
        w_b = jnp.concatenate([w[:, o_qb:o_al], w[:, o_zb:o_gb + dm]], axis=1).astype(BF16)
        w5 = jnp.concatenate([w[:, o_qi:o_qb], w[:, o_al:o_zb],
                              jnp.zeros((d, IDX_GROUP - (o_qb - o_qi) - GATE_RANK), F32)], axis=1)
        w5_hi = w5.astype(BF16)
        w5_lo = (w5 - w5_hi.astype(F32)).astype(BF16)
        w5cat = jnp.concatenate([w5_hi, w5_lo, w5_hi], axis=0)
        w2 = jnp.zeros((LANES, dk), F32).at[MISC_A:MISC_A + GATE_RANK].set(w_alpha2[layer])
        kg = jnp.zeros((1, LANES), F32).at[0, :IDX_DIM].set(kidx_norm_g[layer])
        ba = b_alpha[layer].reshape(1, dk)
        g_pre = pre_norm_g[layer].reshape(1, d)
        gn = gla_norm_g[layer].reshape(1, dv)
        pn = post_norm_g[layer].reshape(1, d)
        woa, wog, wout = (a[layer].astype(BF16) for a in (w_o_attn, w_o_gla, w_out))

        q, kf, kbf, vf, vbf, sza = _proj_attn(xp, g_pre, w_a, cos_p, sin_p, tm, seq // tm)
        (qb, kb, vb, la, szb, sga, sgb, qcat, kcat, misc) = _proj_mix(
            xp, g_pre, w_b, w5cat, kg, w2, ba, cosi_p, sini_p, tm, seq // tm, dk, dv, dm)
        o_attn = _prompt_attn(qcat, misc, kcat, q, kbf, vbf, batch, seq, tq, kblk, top_p)
        o_gla, s_fin = _gla_chunked(qb, kb, vb, la, batch, seq, tt)
        outs["kp"].append(kf.reshape(batch, seq, nh, HEAD_DIM))
        outs["vp"].append(vf.reshape(batch, seq, nh, HEAD_DIM))
        outs["kip"].append(misc[:, :IDX_DIM].reshape(batch, seq, IDX_DIM))
        outs["sp"].append(s_fin)
        xp = _merge(xp, o_attn, sza, o_gla, szb, sga, sgb, gn, woa, wog, wout, pn, tm)

        q_s, kf_s, _, vf_s, _, sza_s = _proj_attn(xs, g_pre, w_a, cos_s, sin_s, db, 1)
        (qb_s, kb_s, vb_s, la_s, szb_s, sga_s, sgb_s, qcat_s, _, misc_s) = _proj_mix(
            xs, g_pre, w_b, w5cat, kg, w2, ba, cosi_s, sini_s, db, 1, dk, dv, dm)
        q8 = qcat_s.reshape(db, IDX_HEADS, 4, IDX_DIM)[:, :, 0::2, :].transpose(0, 2, 1, 3)
        q8 = q8.reshape(db, 2 * IDX_HEADS, IDX_DIM)
        scores, self_sc = _sample_scores(page_table, q8, misc_s, cache_kidx[layer])
        idx = _sample_select(scores, self_sc, top_s)
        o_attn_s = _sample_attn(idx, page_table, q_s.astype(F32).reshape(db, nh, HEAD_DIM),
                                kf_s.reshape(db, nh, HEAD_DIM), vf_s.reshape(db, nh, HEAD_DIM),
                                cache_k[layer], cache_v[layer], top_s)
        o_gla_s, s_new = _gla_step(qb_s, kb_s, vb_s, la_s, state_gla[layer])
        outs["ks"].append(kf_s.reshape(db, 1, nh, HEAD_DIM))
        outs["vs"].append(vf_s.reshape(db, 1, nh, HEAD_DIM))
        outs["kis"].append(misc_s[:, :IDX_DIM].reshape(db, 1, IDX_DIM))
        outs["ss"].append(s_new)
        xs = _merge(xs, o_attn_s.reshape(db, d_attn).astype(BF16), sza_s, o_gla_s, szb_s, sga_s, sgb_s,
                    gn, woa, wog, wout, pn, db)

    st = lambda k: jnp.stack(outs[k])
    return (xp.reshape(batch, seq, d), xs.reshape(db, 1, d), st("kp"), st("vp"), st("kip"), st("sp"),
            st("ks"), st("vs"), st("kis"), st("ss"))
```

```python
import functools

import numpy as np
import jax
import jax.numpy as jnp
from jax import lax
from jax.experimental import pallas as pl
from jax.experimental.pallas import tpu as pltpu

F32 = jnp.float32
BF16 = jnp.bfloat16
I32 = jnp.int32
I16 = jnp.int16

HEAD_DIM = 128
IDX_HEADS = 4
IDX_DIM = 64
TOPK_MAX = 256
GLA_HEADS = 4
GATE_RANK = 16
GATE_TEMP = 16.0
GLA_CHUNK = 64
ROPE_THETA = 10000.0
NORM_EPS = 1e-6
NEG_INF = -1e30
PAGE_SIZE = 128

LOG2E = 1.4426950408889634
Q_SCALE = HEAD_DIM ** -0.5 * LOG2E

LANES = 128
INT_MIN = np.int32(-2147483648)
VMEM_LIMIT = 56 * 1024 * 1024

MISC_W = IDX_DIM
MISC_A = IDX_DIM + IDX_HEADS
IDX_GROUP = 384


def _nt(a, b):
    return lax.dot_general(a, b, (((1,), (1,)), ((), ())), preferred_element_type=F32)


def _rms(x, g):
    var = jnp.mean(x * x, axis=-1, keepdims=True)
    return x * lax.rsqrt(var + NORM_EPS) * g


def _params(sem):
    return pltpu.CompilerParams(dimension_semantics=sem, vmem_limit_bytes=VMEM_LIMIT)


def _proj_attn_kernel(x_ref, g_ref, w_ref, cos_ref, sin_ref, *out_refs, d_attn, prompt):
    h = _rms(x_ref[...], g_ref[...]).astype(BF16)
    cos = cos_ref[...]
    sin = sin_ref[...]
    nh = d_attn // HEAD_DIM

    def rope(z):
        outs = []
        for hd in range(nh):
            zh = z[:, hd * HEAD_DIM:(hd + 1) * HEAD_DIM]
            outs.append(zh * cos + pltpu.roll(zh, HEAD_DIM // 2, axis=1) * sin)
        return jnp.concatenate(outs, axis=1)

    zq = rope(jnp.dot(h, w_ref[:, 0:d_attn], preferred_element_type=F32)) * Q_SCALE
    zk = rope(jnp.dot(h, w_ref[:, d_attn:2 * d_attn], preferred_element_type=F32))
    zv = jnp.dot(h, w_ref[:, 2 * d_attn:3 * d_attn], preferred_element_type=F32)
    zz = jnp.dot(h, w_ref[:, 3 * d_attn:4 * d_attn], preferred_element_type=F32)
    sza = (zz * jax.nn.sigmoid(zz)).astype(BF16)
    if prompt:
        qt_ref, kf_ref, kb_ref, vf_ref, vt_ref, sza_ref = out_refs
        qt_ref[0] = zq.T.astype(BF16)
        kb_ref[...] = zk.astype(BF16)
        vt_ref[0] = zv.T.astype(BF16)
    else:
        q_ref, kf_ref, vf_ref, sza_ref = out_refs
        q_ref[...] = zq.astype(BF16)
    kf_ref[...] = zk
    vf_ref[...] = zv
    sza_ref[...] = sza


def _proj_attn(x, g, w_a, cos, sin, tm, pos_blocks, prompt):
    n, d = x.shape
    d_attn = w_a.shape[1] // 4
    row = lambda i: (i, 0)
    fixed = lambda i: (0, 0)
    posmap = (lambda i: (i % pos_blocks, 0))
    rows = lambda dt: (pl.BlockSpec((tm, d_attn), row), jax.ShapeDtypeStruct((n, d_attn), dt))
    slab = (pl.BlockSpec((1, d_attn, tm), lambda i: (i, 0, 0)), jax.ShapeDtypeStruct((n // tm, d_attn, tm), BF16))
    if prompt:
        outs = [slab, rows(F32), rows(BF16), rows(F32), slab, rows(BF16)]
    else:
        outs = [rows(BF16), rows(F32), rows(F32), rows(BF16)]
    return pl.pallas_call(
        functools.partial(_proj_attn_kernel, d_attn=d_attn, prompt=prompt),
        grid=(n // tm,),
        in_specs=[pl.BlockSpec((tm, d), row), pl.BlockSpec((1, d), fixed),
                  pl.BlockSpec(w_a.shape, fixed, pipeline_mode=pl.Buffered(1)),
                  pl.BlockSpec((tm, LANES), posmap), pl.BlockSpec((tm, LANES), posmap)],
        out_specs=[o[0] for o in outs],
        out_shape=[o[1] for o in outs],
        compiler_params=_params(("parallel",)),
        name="proj_attn",
    )(x, g, w_a, cos, sin)


def _proj_mix_kernel(x_ref, g_ref, wb_ref, w5_ref, kg_ref, w2_ref, ba_ref, cos_ref, sin_ref,
                     qb_ref, kb_ref, vb_ref, la_ref, szb_ref, sga_ref, sgb_ref,
                     qi_ref, kcat_ref, misc_ref, *, dk, dv, dm, prompt):
    hf = _rms(x_ref[...], g_ref[...])
    h_hi = hf.astype(BF16)
    h_lo = (hf - h_hi.astype(F32)).astype(BF16)

    o = 0
    zq = jnp.dot(h_hi, wb_ref[:, o:o + dk], preferred_element_type=F32)
    qb_ref[...] = zq * ((dk // GLA_HEADS) ** -0.5)
    o += dk
    kb_ref[...] = jnp.dot(h_hi, wb_ref[:, o:o + dk], preferred_element_type=F32)
    o += dk
    vb_ref[...] = jnp.dot(h_hi, wb_ref[:, o:o + dv], preferred_element_type=F32)
    o += dv
    zz = jnp.dot(h_hi, wb_ref[:, o:o + dv], preferred_element_type=F32)
    szb_ref[...] = (zz * jax.nn.sigmoid(zz)).astype(BF16)
    o += dv
    sga_ref[...] = jax.nn.sigmoid(jnp.dot(h_hi, wb_ref[:, o:o + dm], preferred_element_type=F32)).astype(BF16)
    o += dm
    sgb_ref[...] = jax.nn.sigmoid(jnp.dot(h_hi, wb_ref[:, o:o + dm], preferred_element_type=F32)).astype(BF16)

    hcat = jnp.concatenate([h_hi, h_hi, h_lo], axis=1)
    z5 = jnp.dot(hcat, w5_ref[...], preferred_element_type=F32)
    cos = cos_ref[...]
    sin = sin_ref[...]
    lane = lax.broadcasted_iota(I32, (1, LANES), 1)
    first_half = (lane & (IDX_DIM - 1)) < (IDX_DIM // 2)
    low64 = lane < IDX_DIM

    def rope64(z):
        rot = jnp.where(first_half, pltpu.roll(z, LANES - IDX_DIM // 2, axis=1),
                        pltpu.roll(z, IDX_DIM // 2, axis=1))
        return z * cos + rot * sin

    def hilo(z):
        hi = z.astype(BF16).astype(F32)
        lo = (z - hi).astype(BF16).astype(F32)
        return hi, lo

    pieces = []
    for p in range(IDX_HEADS * IDX_DIM // LANES):
        zr = rope64(z5[:, p * LANES:(p + 1) * LANES])
        if not prompt:
            pieces.append(zr)
            continue
        sw = pltpu.roll(zr, IDX_DIM, axis=1)
        for dup in (jnp.where(low64, zr, sw), jnp.where(low64, sw, zr)):
            pieces += list(hilo(dup))
    if prompt:
        qi_ref[0] = jnp.concatenate(pieces, axis=1).T.astype(BF16)
    else:
        qi_ref[...] = jnp.concatenate(pieces, axis=1)

    m = z5[:, IDX_GROUP - LANES:IDX_GROUP]
    ki = jnp.where(low64, m, 0.0)
    var = jnp.sum(ki * ki, axis=-1, keepdims=True) * (1.0 / IDX_DIM)
    kin = rope64(ki * lax.rsqrt(var + NORM_EPS) * kg_ref[...])
    is_w = (lane >= MISC_W) & (lane < MISC_A)
    misc = jnp.where(low64, kin, jnp.where(is_w, m * (IDX_HEADS ** -0.5 * IDX_DIM ** -0.5), m))
    misc_ref[...] = misc
    kdup = jnp.where(low64, kin, pltpu.roll(kin, IDX_DIM, axis=1))
    hi, lo = hilo(kdup)
    kpiece = jnp.where(low64, hi, lo).astype(BF16)
    kcat_ref[...] = jnp.concatenate([kpiece, kpiece], axis=1)

    pre = jnp.dot(m, w2_ref[...], preferred_element_type=F32, precision=lax.Precision.HIGHEST) + ba_ref[...]
    la_ref[...] = jax.nn.log_sigmoid(pre) * (1.0 / GATE_TEMP)


def _proj_mix(x, g, w_b, w5, kg, w2, ba, cos, sin, tm, pos_blocks, dk, dv, dm, prompt):
    n, d = x.shape
    row = lambda i: (i, 0)
    fixed = lambda i: (0, 0)
    posmap = (lambda i: (i % pos_blocks, 0))
    rows = lambda w, dt: (pl.BlockSpec((tm, w), row), jax.ShapeDtypeStruct((n, w), dt))
    qcat_w = IDX_HEADS * 4 * IDX_DIM
    if prompt:
        qi = (pl.BlockSpec((1, qcat_w, tm), lambda i: (i, 0, 0)), jax.ShapeDtypeStruct((n // tm, qcat_w, tm), BF16))
    else:
        qi = rows(IDX_HEADS * IDX_DIM, F32)
    outs = [rows(dk, F32), rows(dk, F32), rows(dv, F32), rows(dk, F32), rows(dv, BF16), rows(dm, BF16),
            rows(dm, BF16), qi, rows(4 * IDX_DIM, BF16), rows(LANES, F32)]
    once = lambda a: pl.BlockSpec(a.shape, fixed, pipeline_mode=pl.Buffered(1))
    return pl.pallas_call(
        functools.partial(_proj_mix_kernel, dk=dk, dv=dv, dm=dm, prompt=prompt),
        grid=(n // tm,),
        in_specs=[pl.BlockSpec((tm, d), row), pl.BlockSpec((1, d), fixed),
                  once(w_b), once(w5),
                  pl.BlockSpec((1, LANES), fixed), once(w2),
                  pl.BlockSpec((1, dk), fixed),
                  pl.BlockSpec((tm, LANES), posmap), pl.BlockSpec((tm, LANES), posmap)],
        out_specs=[o[0] for o in outs],
        out_shape=[o[1] for o in outs],
        compiler_params=_params(("parallel",)),
        name="proj_mix",
    )(x, g, w_b, w5, kg, w2, ba, cos, sin)


def _sort_key(score):
    bits = lax.bitcast_convert_type(score, I32)
    key = jnp.where(bits < 0, bits ^ np.int32(0x7FFFFFFF), bits)
    return jnp.where(bits == INT_MIN, np.int32(0), key)


def _kth_largest_key(count_ge, shape, top):
    def body(it, tu):
        cu = tu | lax.shift_left(np.int32(1), np.int32(31) - it)
        cnt = count_ge(cu ^ INT_MIN)
        return jnp.where(cnt >= top, cu, tu)
    tu = lax.fori_loop(0, 32, body, jnp.zeros(shape, I32))
    return tu ^ INT_MIN


def _prompt_attn_kernel(qcat_ref, misc_ref, kcat_ref, qt_ref, k_ref, vt_ref, o_ref,
                        key_ref, hi_ref, lo_ref, acc_ref, m_ref, l_ref, bias_ref, s_ref, eqc_ref,
                        *, tq, kblk, top, nh):
    i = pl.program_id(1)
    nkb = ((i + 1) * tq + kblk - 1) // kblk
    q_pos = i * tq + lax.broadcasted_iota(I32, (1, tq), 1)
    row_k = lax.broadcasted_iota(I32, (kblk, 1), 0)
    w_t = misc_ref[...].T
    pack = 16
    i16 = lambda v: jnp.asarray(v, I32).astype(I16)

    def score_body(kb, c):
        k0 = pl.multiple_of(kb * kblk, kblk)
        kc = kcat_ref[pl.ds(k0, kblk), :]
        sc = jnp.zeros((kblk, tq), F32)
        for h in range(IDX_HEADS):
            lg = jnp.dot(kc, qcat_ref[0, h * 4 * IDX_DIM:(h + 1) * 4 * IDX_DIM, :], preferred_element_type=F32)
            sc = sc + w_t[MISC_W + h:MISC_W + h + 1, :] * jnp.maximum(lg, 0.0)
        key = jnp.where(k0 + row_k <= q_pos, _sort_key(sc), INT_MIN)
        key_ref[kb] = key
        hi_ref[kb] = lax.shift_right_arithmetic(key, 16).astype(I16)
        lo_ref[kb] = ((key & 0xFFFF) - 32768).astype(I16)
        return c

    lax.fori_loop(0, nkb, score_body, 0)

    one, zero = jnp.asarray(1.0, BF16), jnp.asarray(0.0, BF16)

    def count16(ref, pred):
        def body(kb, tot):
            m = jnp.where(pred(ref[kb]), one, zero)
            parts = [m[j * pack:(j + 1) * pack] for j in range(kblk // pack)]
            while len(parts) > 1:
                parts = [parts[j] + parts[j + 1] for j in range(0, len(parts), 2)]
            return tot + parts[0].astype(F32)
        tot = lax.fori_loop(0, nkb, body, jnp.zeros((pack, tq), F32))
        return jnp.sum(tot, axis=0, keepdims=True)

    def kth16(ref, want):
        def body(it, tu):
            cu = tu | lax.shift_left(np.int32(1), np.int32(15) - it)
            c16 = (cu - 32768).astype(I16)
            return jnp.where(count16(ref, lambda x: x >= c16) >= want, cu, tu)
        return lax.fori_loop(0, 16, body, jnp.zeros((1, tq), I32)) - 32768

    t_hi = kth16(hi_ref, float(top))
    t_hi16 = t_hi.astype(I16)
    n_gt = count16(hi_ref, lambda x: x > t_hi16)

    def mask_body(kb, c):
        lo_ref[kb] = jnp.where(hi_ref[kb] == t_hi16, lo_ref[kb], i16(-32768))
        return c
    lax.fori_loop(0, nkb, mask_body, 0)
    t_lo = kth16(lo_ref, top - n_gt)
    t_lo16 = t_lo.astype(I16)
    n_gt = n_gt + count16(lo_ref, lambda x: x > t_lo16)
    thr = lax.shift_left(t_hi, 16) | (t_lo + 32768)
    need = top - n_gt
    n_eq = count16(lo_ref, lambda x: x == t_lo16)
    tied = jnp.max(jnp.where(n_eq != need, 1.0, jnp.where(thr == INT_MIN, 1.0, 0.0))) > 0.5

    m_ref[...] = jnp.full(m_ref.shape, NEG_INF, F32)
    l_ref[...] = jnp.zeros(l_ref.shape, F32)
    acc_ref[...] = jnp.zeros(acc_ref.shape, F32)
    eqc_ref[...] = jnp.zeros(eqc_ref.shape, F32)

    def attn_body(kb, c):
        k0 = pl.multiple_of(kb * kblk, kblk)
        kk = key_ref[kb]

        @pl.when(tied)
        def _():
            tri = jnp.where(lax.broadcasted_iota(I32, (kblk, kblk), 1) <= lax.broadcasted_iota(I32, (kblk, kblk), 0),
                            1.0, 0.0).astype(BF16)
            eq = jnp.where(kk == thr, jnp.where(k0 + row_k <= q_pos, 1.0, 0.0), 0.0)
            pre = jnp.dot(tri, eq.astype(BF16), preferred_element_type=F32) + eqc_ref[...]
            take = jnp.where(kk > thr, 1.0, jnp.where(pre <= need, eq, 0.0))
            bias_ref[...] = jnp.where(take > 0.5, 0.0, NEG_INF)
            eqc_ref[...] = pre[kblk - 1:kblk, :]

        @pl.when(jnp.logical_not(tied))
        def _():
            bias_ref[...] = jnp.where(kk >= thr, 0.0, NEG_INF)

        kblock = k_ref[pl.ds(k0, kblk), :]

        def qk(h):
            hs = slice(h * HEAD_DIM, (h + 1) * HEAD_DIM)
            return jnp.dot(kblock[:, hs], qt_ref[0, hs, :], preferred_element_type=F32) + bias_ref[...]

        s_ref[0] = qk(0)
        for h in range(nh):
            hs = slice(h * HEAD_DIM, (h + 1) * HEAD_DIM)
            if h + 1 < nh:
                s_ref[(h + 1) % 2] = qk(h + 1)
            s = s_ref[h % 2]
            m_old = m_ref[h]
            m_new = jnp.maximum(m_old, jnp.max(s, axis=0, keepdims=True))
            alpha = jnp.exp2(m_old - m_new)
            p = jnp.exp2(s - m_new)
            l_ref[h] = alpha * l_ref[h] + jnp.sum(p, axis=0, keepdims=True)
            acc_ref[hs, :] = alpha * acc_ref[hs, :] + jnp.dot(vt_ref[kb, hs, :], p.astype(BF16),
                                                              preferred_element_type=F32)
            m_ref[h] = m_new
        return c

    lax.fori_loop(0, nkb, attn_body, 0)
    outs = [acc_ref[h * HEAD_DIM:(h + 1) * HEAD_DIM, :] * (1.0 / l_ref[h]) for h in range(nh)]
    o_ref[...] = jnp.concatenate(outs, axis=0).T.astype(o_ref.dtype)


def _prompt_attn(qcat_t, misc, kcat, q_t, k, v_t, batch, seq, tq, kblk, top):
    n, d_attn = k.shape
    nh = d_attn // HEAD_DIM
    nq = seq // tq
    slab_w = q_t.shape[2]
    r = slab_w // tq
    tile = lambda b, i: (b * nq + i, 0)
    sub = lambda b, i: ((b * nq + i) // r, 0, (b * nq + i) % r)
    whole = lambda b, i: (b, 0)
    nkb = seq // kblk
    assert v_t.shape[2] == kblk and slab_w % tq == 0
    return pl.pallas_call(
        functools.partial(_prompt_attn_kernel, tq=tq, kblk=kblk, top=top, nh=nh),
        grid=(batch, nq),
        in_specs=[pl.BlockSpec((1, qcat_t.shape[1], tq), sub), pl.BlockSpec((tq, LANES), tile),
                  pl.BlockSpec((seq, kcat.shape[1]), whole),
                  pl.BlockSpec((1, d_attn, tq), sub),
                  pl.BlockSpec((seq, d_attn), whole),
                  pl.BlockSpec((nkb, d_attn, kblk), lambda b, i: (b, 0, 0))],
        out_specs=pl.BlockSpec((tq, d_attn), tile),
        out_shape=jax.ShapeDtypeStruct((n, d_attn), BF16),
        scratch_shapes=[pltpu.VMEM((nkb, kblk, tq), I32),
                        pltpu.VMEM((nkb, kblk, tq), I16), pltpu.VMEM((nkb, kblk, tq), I16),
                        pltpu.VMEM((d_attn, tq), F32),
                        pltpu.VMEM((nh, 1, tq), F32), pltpu.VMEM((nh, 1, tq), F32),
                        pltpu.VMEM((kblk, tq), F32), pltpu.VMEM((2, kblk, tq), F32), pltpu.VMEM((1, tq), F32)],
        compiler_params=_params(("parallel", "arbitrary")),
        name="prompt_attn",
    )(qcat_t, misc, kcat, q_t, k, v_t)


def _col(row):
    n = row.shape[1]
    eye = lax.broadcasted_iota(I32, (n, n), 0) == lax.broadcasted_iota(I32, (n, n), 1)
    return jnp.sum(jnp.where(eye, row, 0.0), axis=1, keepdims=True)


def _gla_chunk_kernel(q_ref, k_ref, v_ref, la_ref, o_ref, sfin_ref, s_ref, *, tt, chunk):
    t = pl.program_id(2)

    @pl.when(t == 0)
    def _():
        s_ref[...] = jnp.zeros(s_ref.shape, F32)

    r = lax.broadcasted_iota(I32, (tt, tt), 0)
    c = lax.broadcasted_iota(I32, (tt, tt), 1)
    sh = int(np.log2(chunk))
    cum = jnp.where((c <= r) & (jnp.right_shift(c, sh) == jnp.right_shift(r, sh)), 1.0, 0.0).astype(BF16)
    la = la_ref[...]
    la_hi = la.astype(BF16)
    la_mid = (la - la_hi.astype(F32)).astype(BF16)
    la_lo = (la - la_hi.astype(F32) - la_mid.astype(F32)).astype(BF16)
    b = (jnp.dot(cum, la_hi, preferred_element_type=F32) + jnp.dot(cum, la_mid, preferred_element_type=F32)
         + jnp.dot(cum, la_lo, preferred_element_type=F32))
    causal = (lax.broadcasted_iota(I32, (chunk, chunk), 1) <= lax.broadcasted_iota(I32, (chunk, chunk), 0))
    nc = tt // chunk
    rows = [slice(ci * chunk, (ci + 1) * chunk) for ci in range(nc)]
    bl = [b[r_][chunk - 1:chunk] for r_ in rows]
    qd = [(q_ref[r_, :] * jnp.exp(b[r_])).astype(BF16) for r_ in rows]
    kd = [(k_ref[r_, :] * jnp.exp(-b[r_])).astype(BF16) for r_ in rows]
    ks = [(k_ref[r_, :] * jnp.exp(bl[ci] - b[r_])).astype(BF16) for ci, r_ in enumerate(rows)]
    vc = [v_ref[r_, :].astype(BF16) for r_ in rows]
    attn = [jnp.where(causal, _nt(qd[ci], kd[ci]), 0.0).astype(BF16) for ci in range(nc)]
    kv = [lax.dot_general(ks[ci], vc[ci], (((0,), (0,)), ((), ())), preferred_element_type=F32) for ci in range(nc)]
    o_intra = [jnp.dot(attn[ci], vc[ci], preferred_element_type=F32) for ci in range(nc)]
    decay = [jnp.exp(_col(bl[ci])) for ci in range(nc)]
    s = s_ref[...]
    for ci in range(nc):
        o_ref[rows[ci], :] = o_intra[ci] + jnp.dot(qd[ci], s.astype(BF16), preferred_element_type=F32)
        s = decay[ci] * s + kv[ci]
    s_ref[...] = s
    sfin_ref[0, 0] = s


def _gla_chunked(qb, kb, vb, la, batch, seq, tt):
    n, dk = qb.shape
    dv = vb.shape[1]
    dkh, dvh = dk // GLA_HEADS, dv // GLA_HEADS
    nt = seq // tt
    tile = lambda b, h, t: (b * nt + t, h)
    return pl.pallas_call(
        functools.partial(_gla_chunk_kernel, tt=tt, chunk=GLA_CHUNK),
        grid=(batch, GLA_HEADS, nt),
        in_specs=[pl.BlockSpec((tt, dkh), tile), pl.BlockSpec((tt, dkh), tile),
                  pl.BlockSpec((tt, dvh), tile), pl.BlockSpec((tt, dkh), tile)],
        out_specs=[pl.BlockSpec((tt, dvh), tile),
                   pl.BlockSpec((1, 1, dkh, dvh), lambda b, h, t: (b, h, 0, 0))],
        out_shape=[jax.ShapeDtypeStruct((n, dv), F32),
                   jax.ShapeDtypeStruct((batch, GLA_HEADS, dkh, dvh), F32)],
        scratch_shapes=[pltpu.VMEM((dkh, dvh), F32)],
        compiler_params=_params(("parallel", "parallel", "arbitrary")),
        name="gla_chunked",
    )(qb, kb, vb, la)


def _gla_step_kernel(q_ref, k_ref, v_ref, la_ref, s0_ref, o_ref, s_ref, *, dkh, dvh):
    for h in range(GLA_HEADS):
        ks = slice(h * dkh, (h + 1) * dkh)
        vs = slice(h * dvh, (h + 1) * dvh)
        s = jnp.exp(_col(la_ref[0, :, ks])) * s0_ref[0, h] + _col(k_ref[0, :, ks]) * v_ref[0, :, vs]
        s_ref[0, h] = s
        o_ref[0, :, vs] = jnp.sum(_col(q_ref[0, :, ks]) * s, axis=0, keepdims=True)


def _gla_step(qb, kb, vb, la, s0):
    db, dk = qb.shape
    dv = vb.shape[1]
    dkh, dvh = dk // GLA_HEADS, dv // GLA_HEADS
    vec = lambda w: pl.BlockSpec((1, 1, w), lambda b: (b, 0, 0))
    st = pl.BlockSpec((1, GLA_HEADS, dkh, dvh), lambda b: (b, 0, 0, 0))
    r3 = lambda a: a.reshape(db, 1, a.shape[1])
    o, s = pl.pallas_call(
        functools.partial(_gla_step_kernel, dkh=dkh, dvh=dvh),
        grid=(db,),
        in_specs=[vec(dk), vec(dk), vec(dv), vec(dk), st],
        out_specs=[vec(dv), st],
        out_shape=[jax.ShapeDtypeStruct((db, 1, dv), F32), jax.ShapeDtypeStruct(s0.shape, F32)],
        compiler_params=_params(("parallel",)),
        name="gla_step",
    )(r3(qb), r3(kb), r3(vb), r3(la), s0)
    return o.reshape(db, dv), s


def _merge_kernel(x_ref, oa_ref, sza_ref, og_ref, szb_ref, sga_ref, sgb_ref,
                  gn_ref, woa_ref, wog_ref, wout_ref, pn_ref, y_ref, *, dvh):
    a_in = oa_ref[...] * sza_ref[...]
    br_a = jnp.dot(a_in, woa_ref[...], preferred_element_type=F32)
    og = og_ref[...]
    gn = gn_ref[...]
    parts = []
    for h in range(GLA_HEADS):
        vs = slice(h * dvh, (h + 1) * dvh)
        parts.append(_rms(og[:, vs], gn[:, vs]))
    b_in = (jnp.concatenate(parts, axis=1) * szb_ref[...].astype(F32)).astype(BF16)
    br_b = jnp.dot(b_in, wog_ref[...], preferred_element_type=F32)
    mixed = sga_ref[...].astype(F32) * br_a + sgb_ref[...].astype(F32) * br_b
    y = jnp.dot(mixed.astype(BF16), wout_ref[...], preferred_element_type=F32)
    y_ref[...] = x_ref[...] + _rms(y, pn_ref[...])


def _merge(x, oa, sza, og, szb, sga, sgb, gn, woa, wog, wout, pn, tm):
    n, d = x.shape
    row = lambda w: pl.BlockSpec((tm, w), lambda i: (i, 0))
    fixed = lambda a: pl.BlockSpec(a.shape, lambda i: (0, 0))
    return pl.pallas_call(
        functools.partial(_merge_kernel, dvh=og.shape[1] // GLA_HEADS),
        grid=(n // tm,),
        in_specs=[row(d), row(oa.shape[1]), row(sza.shape[1]), row(og.shape[1]), row(szb.shape[1]),
                  row(d), row(d), fixed(gn), fixed(woa), fixed(wog), fixed(wout), fixed(pn)],
        out_specs=row(d),
        out_shape=jax.ShapeDtypeStruct((n, d), F32),
        compiler_params=_params(("parallel",)),
        name="merge",
    )(x, oa, sza, og, szb, sga, sgb, gn, woa, wog, wout, pn)


def _sample_score_kernel(pt_ref, qcol_ref, kself_ref, misc_ref, kidx_hbm, sc_ref, self_ref, kbuf, sem,
                         *, n_pages, chunk_pages):
    b = pl.program_id(0)
    nb = pl.num_programs(0)

    def page_copy(bb, p, slot):
        return pltpu.make_async_copy(kidx_hbm.at[pt_ref[bb, p]], kbuf.at[slot, p], sem.at[slot])

    def issue(bb, slot):
        def body(p, c):
            page_copy(bb, p, slot).start()
            return c
        lax.fori_loop(0, n_pages, body, 0)

    slot = b % 2

    @pl.when(b == 0)
    def _():
        issue(b, slot)

    @pl.when(b + 1 < nb)
    def _():
        issue(b + 1, 1 - slot)

    def wait_body(p, c):
        page_copy(b, p, slot).wait()
        return c
    lax.fori_loop(0, n_pages, wait_body, 0)

    qc = qcol_ref[0]
    wrow = misc_ref[0]
    w = [wrow[:, MISC_W + h:MISC_W + h + 1] for h in range(IDX_HEADS)]

    for c in range(n_pages // chunk_pages):
        kt = kbuf[slot, c * chunk_pages:(c + 1) * chunk_pages]
        sc = jnp.zeros((chunk_pages, PAGE_SIZE), F32)
        for h in range(IDX_HEADS):
            lg = jnp.sum(kt * qc[:, h:h + 1][None], axis=1)
            sc = sc + w[h] * jnp.maximum(lg, 0.0)
        sc_ref[0, c * chunk_pages:(c + 1) * chunk_pages, :] = sc
    ss = jnp.zeros((1, 1), F32)
    for h in range(IDX_HEADS):
        lg = jnp.sum(kself_ref[0] * qc[:, h:h + 1], axis=0, keepdims=True)
        ss = ss + w[h] * jnp.maximum(lg, 0.0)
    self_ref[0] = jnp.broadcast_to(ss, (1, LANES))


def _sample_scores(page_table, qcol, kself, misc, cache_kidx_t):
    db, n_pages = page_table.shape
    chunk_pages = 8 if n_pages % 8 == 0 else n_pages
    grid_spec = pltpu.PrefetchScalarGridSpec(
        num_scalar_prefetch=1,
        grid=(db,),
        in_specs=[pl.BlockSpec((1, IDX_DIM, IDX_HEADS), lambda b, pt: (b, 0, 0)),
                  pl.BlockSpec((1, IDX_DIM, 1), lambda b, pt: (b, 0, 0)),
                  pl.BlockSpec((1, 1, LANES), lambda b, pt: (b, 0, 0)),
                  pl.BlockSpec(memory_space=pl.ANY)],
        out_specs=[pl.BlockSpec((1, n_pages, PAGE_SIZE), lambda b, pt: (b, 0, 0)),
                   pl.BlockSpec((1, 1, LANES), lambda b, pt: (b, 0, 0))],
        scratch_shapes=[pltpu.VMEM((2, n_pages, IDX_DIM, PAGE_SIZE), F32),
                        pltpu.SemaphoreType.DMA((2,))],
    )
    return pl.pallas_call(
        functools.partial(_sample_score_kernel, n_pages=n_pages, chunk_pages=chunk_pages),
        grid_spec=grid_spec,
        out_shape=[jax.ShapeDtypeStruct((db, n_pages, PAGE_SIZE), F32), jax.ShapeDtypeStruct((db, 1, LANES), F32)],
        compiler_params=_params(("arbitrary",)),
        name="sample_scores",
    )(page_table, qcol, kself, misc.reshape(db, 1, LANES), cache_kidx_t)


def _sample_select_kernel(sc_ref, self_ref, idx_ref, keys_ref, thr_ref, kself_ref, *, top, past):
    db, rows, cols = sc_ref.shape
    keys_ref[...] = _sort_key(sc_ref[...])
    kself = _sort_key(self_ref[...][:, :, 0:1])
    kself_ref[...] = kself

    def count_ge(c):
        m = jnp.where(keys_ref[...] >= c, 1, 0)
        tot = jnp.sum(jnp.sum(m, axis=1, keepdims=True), axis=2, keepdims=True)
        return tot + jnp.where(kself >= c, 1, 0)

    thr_ref[...] = _kth_largest_key(count_ge, (db, 1, 1), top)

    def upper(n):
        return jnp.where(lax.broadcasted_iota(I32, (n, n), 0) <= lax.broadcasted_iota(I32, (n, n), 1),
                         1.0, 0.0).astype(BF16)
    tri_c = upper(cols)
    tri_r = upper(rows)
    low = jnp.where(lax.broadcasted_iota(I32, (rows, rows), 1) < lax.broadcasted_iota(I32, (rows, rows), 0),
                    1.0, 0.0).astype(BF16)
    slot = lax.broadcasted_iota(I32, (top, 1), 0).astype(F32)
    row_of = lax.broadcasted_iota(I32, (1, rows), 1).astype(F32)
    ones8 = jnp.ones((8, cols), BF16)

    def prefix(mask_bf):
        within = jnp.dot(mask_bf, tri_c, preferred_element_type=F32)
        before = jnp.sum(jnp.dot(low, mask_bf, preferred_element_type=F32), axis=1, keepdims=True)
        return within + before

    def body(b, c):
        kk = keys_ref[b]
        thr = thr_ref[b]
        gt = kk > thr
        eq = kk == thr
        n_gt = jnp.sum(jnp.sum(jnp.where(gt, 1.0, 0.0), axis=1, keepdims=True), axis=0, keepdims=True)
        n_gt = n_gt + jnp.where(kself_ref[b] > thr, 1.0, 0.0)
        need = top - n_gt
        eq_bf = jnp.where(eq, 1.0, 0.0).astype(BF16)
        sel = gt | (eq & (prefix(eq_bf) <= need))
        sel_bf = jnp.where(sel, 1.0, 0.0).astype(BF16)
        g = prefix(sel_bf)
        n_sel = g[rows - 1:rows, cols - 1:cols]
        row_tot = _nt(ones8, sel_bf)[0:1]
        row_cum = jnp.dot(row_tot.astype(BF16), tri_r, preferred_element_type=F32)
        page = jnp.sum(jnp.where(row_cum <= slot, 1.0, 0.0), axis=1, keepdims=True)
        onehot = jnp.where(row_of == page, 1.0, 0.0).astype(BF16)
        g_row = jnp.dot(onehot, g.astype(BF16), preferred_element_type=F32)
        off = jnp.sum(jnp.where(g_row <= slot, 1.0, 0.0), axis=1, keepdims=True)
        idx = page * float(cols) + off
        idx = jnp.where(slot < n_sel, idx, float(past))
        idx_ref[b] = jnp.broadcast_to(idx, (top, LANES)).T[0:8].astype(I32)
        return c

    lax.fori_loop(0, db, body, 0)


def _sample_select(scores, self_sc, top):
    db, rows, cols = scores.shape
    past = rows * cols
    sc3 = scores
    out = pl.pallas_call(
        functools.partial(_sample_select_kernel, top=top, past=past),
        in_specs=[pl.BlockSpec(sc3.shape, lambda: (0, 0, 0)), pl.BlockSpec(self_sc.shape, lambda: (0, 0, 0))],
        out_specs=pl.BlockSpec((db, 8, top), lambda: (0, 0, 0)),
        out_shape=jax.ShapeDtypeStruct((db, 8, top), I32),
        scratch_shapes=[pltpu.VMEM(sc3.shape, I32), pltpu.VMEM((db, 1, 1), I32), pltpu.VMEM((db, 1, 1), I32)],
        compiler_params=pltpu.CompilerParams(vmem_limit_bytes=VMEM_LIMIT),
        name="sample_select",
    )(sc3, self_sc)
    return out[:, 0, :]


def _sample_attn_kernel(idx_ref, pt_ref, q_ref, knew_ref, vnew_ref, ck_hbm, cv_hbm, o_ref,
                        kbuf, vbuf, sem, *, top, past):
    b = pl.program_id(0)
    nb = pl.num_programs(0)
    page_shift = PAGE_SIZE.bit_length() - 1

    def row_copies(row, slot, j):
        return (pltpu.make_async_copy(ck_hbm.at[row], kbuf.at[slot, j], sem.at[slot]),
                pltpu.make_async_copy(cv_hbm.at[row], vbuf.at[slot, j], sem.at[slot]))

    def issue(bb, slot):
        def body(j, c):
            ix = jnp.minimum(idx_ref[bb, j], past - 1)
            phys = pt_ref[bb, lax.shift_right_logical(ix, page_shift)]
            ck, cv = row_copies(phys * PAGE_SIZE + (ix & (PAGE_SIZE - 1)), slot, j)
            ck.start()
            cv.start()
            return c
        lax.fori_loop(0, top, body, 0, unroll=4)

    slot = b % 2

    @pl.when(b == 0)
    def _():
        issue(b, slot)

    @pl.when(b + 1 < nb)
    def _():
        issue(b + 1, 1 - slot)

    def wait_body(j, c):
        ck, cv = row_copies(0, slot, j)
        ck.wait()
        cv.wait()
        return c
    lax.fori_loop(0, top, wait_body, 0)

    @pl.when(idx_ref[b, top - 1] >= past)
    def _():
        kbuf[slot, top - 1] = knew_ref[0]
        vbuf[slot, top - 1] = vnew_ref[0]

    q = q_ref[0]
    s = jnp.sum(kbuf[slot] * q[None], axis=2, keepdims=True)
    m = jnp.max(s, axis=0, keepdims=True)
    p = jnp.exp2(s - m)
    l = jnp.sum(p, axis=0, keepdims=True)
    o = jnp.sum(p * vbuf[slot], axis=0)
    o_ref[0] = o / l[0]


def _sample_attn(idx, page_table, q, k_new, v_new, cache_k, cache_v, top):
    db, nh, hd = q.shape
    past = page_table.shape[1] * PAGE_SIZE
    vec = pl.BlockSpec((1, nh, hd), lambda b, ix, pt: (b, 0, 0))
    grid_spec = pltpu.PrefetchScalarGridSpec(
        num_scalar_prefetch=2,
        grid=(db,),
        in_specs=[vec, vec, vec, pl.BlockSpec(memory_space=pl.ANY), pl.BlockSpec(memory_space=pl.ANY)],
        out_specs=vec,
        scratch_shapes=[pltpu.VMEM((2, top, nh, hd), F32), pltpu.VMEM((2, top, nh, hd), F32),
                        pltpu.SemaphoreType.DMA((2,))],
    )
    return pl.pallas_call(
        functools.partial(_sample_attn_kernel, top=top, past=past),
        grid_spec=grid_spec,
        out_shape=jax.ShapeDtypeStruct((db, nh, hd), F32),
        compiler_params=_params(("arbitrary",)),
        name="sample_attn",
    )(idx, page_table, q, k_new, v_new, cache_k, cache_v)


def _rope_tables(pos, dim):
    half = dim // 2
    inv_freq = ROPE_THETA ** (-jnp.arange(half, dtype=F32) / half)
    ang = pos.astype(F32)[:, None] * inv_freq[None, :]
    cos, sin = jnp.cos(ang), jnp.sin(ang)
    rep = LANES // dim
    return (jnp.tile(jnp.concatenate([cos, cos], axis=1), (1, rep)),
            jnp.tile(jnp.concatenate([-sin, sin], axis=1), (1, rep)))


def _pick_tile(n, pref):
    t = min(n, pref)
    while n % t:
        t //= 2
    return t


def kernel(x_prompt, x_sample, cache_k, cache_v, cache_kidx, state_gla, page_table, pre_norm_g, w_in,
           kidx_norm_g, w_alpha2, b_alpha, gla_norm_g, w_o_attn, w_o_gla, w_out, post_norm_g):
    batch, seq, d = x_prompt.shape
    db, dseq, _ = x_sample.shape
    assert dseq == 1, "sample group is one token per sequence"
    depth = w_in.shape[0]
    n_pages = page_table.shape[1]
    past = n_pages * PAGE_SIZE
    d_attn = w_o_attn.shape[1]
    nh = d_attn // HEAD_DIM
    dk = w_alpha2.shape[2]
    dv = w_o_gla.shape[1]
    dm = d
    nqi = IDX_HEADS * IDX_DIM

    pos_p = jnp.arange(seq, dtype=jnp.int32)
    pos_s = jnp.full((db,), past, dtype=jnp.int32)
    cos_p, sin_p = _rope_tables(pos_p, HEAD_DIM)
    cosi_p, sini_p = _rope_tables(pos_p, IDX_DIM)
    cos_s, sin_s = _rope_tables(pos_s, HEAD_DIM)
    cosi_s, sini_s = _rope_tables(pos_s, IDX_DIM)

    kblk = _pick_tile(seq, 512)
    tm = kblk
    tq = _pick_tile(seq, 256)
    tt = _pick_tile(seq, 512)
    top_p = min(TOPK_MAX, seq // 4)
    top_s = min(TOPK_MAX, (past + 1) // 4)

    xp, xs = x_prompt.reshape(batch * seq, d), x_sample.reshape(db, d)
    outs = {k: [] for k in ("kp", "vp", "kip", "sp", "ks", "vs", "kis", "ss")}
    for layer in range(depth):
        w = w_in[layer]
        o_qi = 4 * d_attn
        o_ki = o_qi + nqi
        o_wi = o_ki + IDX_DIM
        o_qb = o_wi + IDX_HEADS
        o_kb = o_qb + dk
        o_vb = o_kb + dk
        o_al = o_vb + dv
        o_zb = o_al + GATE_RANK
        o_ga = o_zb + dv
        o_gb = o_ga + dm
        w_a = w[:, :o_qi].astype(BF16)
        w_b = jnp.concatenate([w[:, o_qb:o_al], w[:, o_zb:o_gb + dm]], axis=1).astype(BF16)
        w5 = jnp.concatenate([w[:, o_qi:o_qb], w[:, o_al:o_zb],
                              jnp.zeros((d, IDX_GROUP - (o_qb - o_qi) - GATE_RANK), F32)], axis=1)
        w5_hi = w5.astype(BF16)
        w5_lo = (w5 - w5_hi.astype(F32)).astype(BF16)
        w5cat = jnp.concatenate([w5_hi, w5_lo, w5_hi], axis=0)
        w2 = jnp.zeros((LANES, dk), F32).at[MISC_A:MISC_A + GATE_RANK].set(w_alpha2[layer])
        kg = jnp.zeros((1, LANES), F32).at[0, :IDX_DIM].set(kidx_norm_g[layer])
        ba = b_alpha[layer].reshape(1, dk)
        g_pre = pre_norm_g[layer].reshape(1, d)
        gn = gla_norm_g[layer].reshape(1, dv)
        pn = post_norm_g[layer].reshape(1, d)
        woa, wog, wout = (a[layer].astype(BF16) for a in (w_o_attn, w_o_gla, w_out))

        q_t, kf, kbf, vf, v_t, sza = _proj_attn(xp, g_pre, w_a, cos_p, sin_p, tm, seq // tm, True)
        (qb, kb, vb, la, szb, sga, sgb, qcat_t, kcat, misc) = _proj_mix(
            xp, g_pre, w_b, w5cat, kg, w2, ba, cosi_p, sini_p, tm, seq // tm, dk, dv, dm, True)
        o_attn = _prompt_attn(qcat_t, misc, kcat, q_t, kbf, v_t, batch, seq, tq, kblk, top_p)
        o_gla, s_fin = _gla_chunked(qb, kb, vb, la, batch, seq, tt)
        outs["kp"].append(kf.reshape(batch, seq, nh, HEAD_DIM))
        outs["vp"].append(vf.reshape(batch, seq, nh, HEAD_DIM))
        outs["kip"].append(misc[:, :IDX_DIM].reshape(batch, seq, IDX_DIM))
        outs["sp"].append(s_fin)
        xp = _merge(xp, o_attn, sza, o_gla, szb, sga, sgb, gn, woa, wog, wout, pn, tm)

        q_s, kf_s, vf_s, sza_s = _proj_attn(xs, g_pre, w_a, cos_s, sin_s, db, 1, False)
        (qb_s, kb_s, vb_s, la_s, szb_s, sga_s, sgb_s, qi_s, _, misc_s) = _proj_mix(
            xs, g_pre, w_b, w5cat, kg, w2, ba, cosi_s, sini_s, db, 1, dk, dv, dm, False)
        qcol = qi_s.reshape(db, IDX_HEADS, IDX_DIM).transpose(0, 2, 1)
        kself = misc_s[:, :IDX_DIM].reshape(db, IDX_DIM, 1)
        scores, self_sc = _sample_scores(page_table, qcol, kself, misc_s,
                                         jnp.swapaxes(cache_kidx[layer], 1, 2))
        idx = _sample_select(scores, self_sc, top_s)
        rows = lambda c: c.reshape(c.shape[0] * PAGE_SIZE, nh, HEAD_DIM)
        o_attn_s = _sample_attn(idx, page_table, q_s.astype(F32).reshape(db, nh, HEAD_DIM),
                                kf_s.reshape(db, nh, HEAD_DIM), vf_s.reshape(db, nh, HEAD_DIM),
                                rows(cache_k[layer]), rows(cache_v[layer]), top_s)
        o_gla_s, s_new = _gla_step(qb_s, kb_s, vb_s, la_s, state_gla[layer])
        outs["ks"].append(kf_s.reshape(db, 1, nh, HEAD_DIM))
        outs["vs"].append(vf_s.reshape(db, 1, nh, HEAD_DIM))
        outs["kis"].append(misc_s[:, :IDX_DIM].reshape(db, 1, IDX_DIM))
        outs["ss"].append(s_new)
        xs = _merge(xs, o_attn_s.reshape(db, d_attn).astype(BF16), sza_s, o_gla_s, szb_s, sga_s, sgb_s,
                    gn, woa, wog, wout, pn, db)

    st = lambda k: jnp.stack(outs[k])
    return (xp.reshape(batch, seq, d), xs.reshape(db, 1, d), st("kp"), st("vp"), st("kip"), st("sp"),
            st("ks"), st("vs"), st("kis"), st("ss"))
```
